```python
import jax, jax.numpy as jnp
from jax import lax
import numpy as np

D_MODEL = 1024
BATCH = 16
SEQ = 2048
DEPTH = 1
DEC_BATCH = 8
DEC_SEQ = 2048
PAST_LEN = 128

MEM_LEN = 256
MLA_HEADS = 8
QK_NOPE_DIM = 64
QK_ROPE_DIM = 32
V_HEAD_DIM = 64
Q_LORA_RANK = 384
KV_LORA_RANK = 256
MLA_WIDTH = MLA_HEADS * V_HEAD_DIM
FNET_GROUPS = 4
FNET_GROUP_DIM = 128
FNET_WIDTH = FNET_GROUPS * FNET_GROUP_DIM
MIX_WIDTH = MLA_WIDTH + FNET_WIDTH
IN_WIDTH = Q_LORA_RANK + KV_LORA_RANK + QK_ROPE_DIM + FNET_WIDTH
XATTN_HEADS = 4
XATTN_HEAD_DIM = D_MODEL // XATTN_HEADS
D_FF = -(-8 * D_MODEL // (3 * 256)) * 256
ROPE_BASE = 10000.0
Q_BLOCK = 128
NORM_EPS = 1e-6

kernel_name = "hymba_mla_fnet_memory_encoder"


def rmsnorm(x, g):
    xf = x.astype(jnp.float32)
    y = xf * lax.rsqrt(jnp.mean(xf * xf, axis=-1, keepdims=True) + NORM_EPS)
    return (y * g.astype(jnp.float32)).astype(x.dtype)


def rope_tables(seq):
    inv = 1.0 / (ROPE_BASE ** (jnp.arange(0, QK_ROPE_DIM, 2, dtype=jnp.float32) / QK_ROPE_DIM))
    ang = jnp.arange(seq, dtype=jnp.float32)[:, None] * inv[None, :]
    return jnp.cos(ang), jnp.sin(ang)


def apply_rope(x, cos, sin):
    xf = x.astype(jnp.float32)
    half = QK_ROPE_DIM // 2
    x1, x2 = xf[..., :half], xf[..., half:]
    return jnp.concatenate([x1 * cos - x2 * sin, x2 * cos + x1 * sin], axis=-1).astype(x.dtype)


def mla(c_q, c_kv, k_rope, q_norm_g, w_uq, kv_norm_g, w_ukv):
    B, S, _ = c_q.shape
    H = MLA_HEADS
    q = jnp.einsum('bsr,rn->bsn', rmsnorm(c_q, q_norm_g), w_uq).reshape(B, S, H, QK_NOPE_DIM + QK_ROPE_DIM)
    q_nope, q_rope = q[..., :QK_NOPE_DIM], q[..., QK_NOPE_DIM:]
    kv = jnp.einsum('bsr,rn->bsn', rmsnorm(c_kv, kv_norm_g), w_ukv).reshape(B, S, H, QK_NOPE_DIM + V_HEAD_DIM)
    k_nope, v = kv[..., :QK_NOPE_DIM], kv[..., QK_NOPE_DIM:]
    cos, sin = rope_tables(S)
    q_rope = apply_rope(q_rope, cos[:, None, :], sin[:, None, :])
    k_rope = apply_rope(k_rope, cos, sin)
    nb = S // Q_BLOCK
    qn = q_nope.reshape(B, nb, Q_BLOCK, H, QK_NOPE_DIM).swapaxes(0, 1)
    qr = q_rope.reshape(B, nb, Q_BLOCK, H, QK_ROPE_DIM).swapaxes(0, 1)
    scale = (QK_NOPE_DIM + QK_ROPE_DIM) ** -0.5

    def query_block(args):
        qn_b, qr_b = args
        s = (jnp.einsum('bqhd,bkhd->bhqk', qn_b, k_nope, preferred_element_type=jnp.float32)
             + jnp.einsum('bqhr,bkr->bhqk', qr_b, k_rope, preferred_element_type=jnp.float32))
        p = jax.nn.softmax(s * scale, axis=-1).astype(v.dtype)
        return jnp.einsum('bhqk,bkhd->bqhd', p, v)

    o = lax.map(query_block, (qn, qr))
    return o.swapaxes(0, 1).reshape(B, S, H * V_HEAD_DIM)


def fourier_mix(u, w_fnet):
    B, S, _ = u.shape
    ug = u.reshape(B, S, FNET_GROUPS, FNET_GROUP_DIM).astype(jnp.float32)
    f = jnp.fft.fftn(ug, axes=(1, 3), norm='ortho').real.astype(u.dtype)
    return jnp.einsum('bsgc,gcd->bsgd', f, w_fnet).reshape(B, S, FNET_WIDTH)


def memory_xattn(h, mem, mem_norm_g, w_xq, w_xkv, w_xo):
    B, S, _ = h.shape
    M = mem.shape[1]
    m = rmsnorm(mem, mem_norm_g)
    q = jnp.einsum('bsd,dn->bsn', h, w_xq).reshape(B, S, XATTN_HEADS, XATTN_HEAD_DIM)
    kv = jnp.einsum('bmd,dn->bmn', m, w_xkv).reshape(B, M, 2, XATTN_HEADS, XATTN_HEAD_DIM)
    k, v = kv[:, :, 0], kv[:, :, 1]
    s = jnp.einsum('bqhd,bkhd->bhqk', q, k, preferred_element_type=jnp.float32) * (XATTN_HEAD_DIM ** -0.5)
    p = jax.nn.softmax(s, axis=-1).astype(v.dtype)
    o = jnp.einsum('bhqk,bkhd->bqhd', p, v).reshape(B, S, D_MODEL)
    return jnp.einsum('bsn,nd->bsd', o, w_xo)


def encoder_layer(x, mem, ln_mix_g, w_in, q_norm_g, w_uq, kv_norm_g, w_ukv, w_fnet, w_out,
                  ln_x_g, mem_norm_g, w_xq, w_xkv, w_xo, ln_ffn_g, w_gate_up, w_down):
    h = rmsnorm(x, ln_mix_g)
    z = jnp.einsum('bsd,dn->bsn', h, w_in)
    o1 = Q_LORA_RANK
    o2 = o1 + KV_LORA_RANK
    o3 = o2 + QK_ROPE_DIM
    c_q, c_kv, k_rope, u = z[..., :o1], z[..., o1:o2], z[..., o2:o3], z[..., o3:]
    y_mix = jnp.concatenate([mla(c_q, c_kv, k_rope, q_norm_g, w_uq, kv_norm_g, w_ukv),
                             fourier_mix(u, w_fnet)], axis=-1)
    x = x + jnp.einsum('bsn,nd->bsd', y_mix, w_out)
    x = x + memory_xattn(rmsnorm(x, ln_x_g), mem, mem_norm_g, w_xq, w_xkv, w_xo)
    gu = jnp.einsum('bsd,dn->bsn', rmsnorm(x, ln_ffn_g), w_gate_up)
    g, up = gu[..., :D_FF], gu[..., D_FF:]
    return x + jnp.einsum('bsf,fd->bsd', jax.nn.silu(g) * up, w_down)


def trunk(x, mem, ln_mix_g, w_in, q_norm_g, w_uq, kv_norm_g, w_ukv, w_fnet, w_out,
          ln_x_g, mem_norm_g, w_xq, w_xkv, w_xo, ln_ffn_g, w_gate_up, w_down, final_norm_g):
    for l in range(DEPTH):
        x = encoder_layer(x, mem, ln_mix_g[l], w_in[l], q_norm_g[l], w_uq[l], kv_norm_g[l], w_ukv[l],
                          w_fnet[l], w_out[l], ln_x_g[l], mem_norm_g[l], w_xq[l], w_xkv[l], w_xo[l],
                          ln_ffn_g[l], w_gate_up[l], w_down[l])
    return rmsnorm(x, final_norm_g)


def setup_inputs(seed: int = 0) -> dict:
    key = jax.random.key(seed)
    ks = jax.random.split(key, 24)
    f32 = jnp.float32

    def w(k, shape, fan_in):
        return jax.random.normal(k, shape, f32) * (fan_in ** -0.5)

    def gain(k, shape):
        return 1.0 + 0.01 * jax.random.normal(k, shape, f32)

    L = DEPTH
    return {
        "x_prompt": jax.random.normal(ks[0], (BATCH, SEQ, D_MODEL), f32),
        "x_sample": jax.random.normal(ks[1], (DEC_BATCH, DEC_SEQ, D_MODEL), f32),
        "mem_prompt": jax.random.normal(ks[2], (BATCH, MEM_LEN, D_MODEL), f32),
        "mem_sample": jax.random.normal(ks[3], (DEC_BATCH, MEM_LEN, D_MODEL), f32),
        "ln_mix_g": gain(ks[4], (L, D_MODEL)),
        "w_in": w(ks[5], (L, D_MODEL, IN_WIDTH), D_MODEL),
        "q_norm_g": gain(ks[6], (L, Q_LORA_RANK)),
        "w_uq": w(ks[7], (L, Q_LORA_RANK, MLA_HEADS * (QK_NOPE_DIM + QK_ROPE_DIM)), Q_LORA_RANK),
        "kv_norm_g": gain(ks[8], (L, KV_LORA_RANK)),
        "w_ukv": w(ks[9], (L, KV_LORA_RANK, MLA_HEADS * (QK_NOPE_DIM + V_HEAD_DIM)), KV_LORA_RANK),
        "w_fnet": w(ks[10], (L, FNET_GROUPS, FNET_GROUP_DIM, FNET_GROUP_DIM), FNET_GROUP_DIM),
        "w_out": w(ks[11], (L, MIX_WIDTH, D_MODEL), MIX_WIDTH),
        "ln_x_g": gain(ks[12], (L, D_MODEL)),
        "mem_norm_g": gain(ks[13], (L, D_MODEL)),
        "w_xq": w(ks[14], (L, D_MODEL, D_MODEL), D_MODEL),
        "w_xkv": w(ks[15], (L, D_MODEL, 2 * D_MODEL), D_MODEL),
        "w_xo": w(ks[16], (L, D_MODEL, D_MODEL), D_MODEL),
        "ln_ffn_g": gain(ks[17], (L, D_MODEL)),
        "w_gate_up": w(ks[18], (L, D_MODEL, 2 * D_FF), D_MODEL),
        "w_down": w(ks[19], (L, D_FF, D_MODEL), D_FF),
        "final_norm_g": gain(ks[20], (D_MODEL,)),
    }


def reference(x_prompt, x_sample, mem_prompt, mem_sample, ln_mix_g, w_in, q_norm_g, w_uq, kv_norm_g, w_ukv,
              w_fnet, w_out, ln_x_g, mem_norm_g, w_xq, w_xkv, w_xo, ln_ffn_g, w_gate_up, w_down, final_norm_g):
    y_prompt = trunk(x_prompt, mem_prompt, ln_mix_g, w_in, q_norm_g, w_uq, kv_norm_g, w_ukv, w_fnet, w_out,
                     ln_x_g, mem_norm_g, w_xq, w_xkv, w_xo, ln_ffn_g, w_gate_up, w_down, final_norm_g)
    y_sample = trunk(x_sample, mem_sample, ln_mix_g, w_in, q_norm_g, w_uq, kv_norm_g, w_ukv, w_fnet, w_out,
                     ln_x_g, mem_norm_g, w_xq, w_xkv, w_xo, ln_ffn_g, w_gate_up, w_down, final_norm_g)
    return (y_prompt, y_sample)
```

```python
import functools

import jax
import jax.numpy as jnp
from jax import lax
from jax.experimental import pallas as pl
from jax.experimental.pallas import tpu as pltpu

F32 = jnp.float32
BF16 = jnp.bfloat16

D_MODEL = 1024
MLA_HEADS = 8
QK_NOPE_DIM = 64
QK_ROPE_DIM = 32
V_HEAD_DIM = 64
Q_LORA_RANK = 384
KV_LORA_RANK = 256
FNET_GROUPS = 4
FNET_GROUP_DIM = 128
FNET_WIDTH = FNET_GROUPS * FNET_GROUP_DIM
MLA_WIDTH = MLA_HEADS * V_HEAD_DIM
XATTN_HEADS = 4
XATTN_HEAD_DIM = D_MODEL // XATTN_HEADS
D_FF = 2816
ROPE_BASE = 10000.0
NORM_EPS = 1e-6

LANES = 128
HEAD_PAD = LANES
FF_CHUNK = 256
VMEM_LIMIT = 56 * 1024 * 1024
TILES = (512, 512, 256)

_O_CQ = 0
_O_CKV = _O_CQ + Q_LORA_RANK
_O_KR = _O_CKV + KV_LORA_RANK
_O_KRR = _O_KR + HEAD_PAD
_O_U = _O_KRR + HEAD_PAD
IN_EXT = _O_U + FNET_WIDTH


def _rms(x, g):
    return x * lax.rsqrt(jnp.mean(x * x, axis=-1, keepdims=True) + NORM_EPS) * g


def _dot(a, b):
    return jnp.dot(a, b, preferred_element_type=F32)


def _dot_nt(a, b):
    return lax.dot_general(a, b, (((1,), (1,)), ((), ())), preferred_element_type=F32)


def _const_spec(shape):
    zeros = (0,) * len(shape)
    return pl.BlockSpec(shape, lambda *_: zeros, pipeline_mode=pl.Buffered(1))


def _params(n_axes):
    return pltpu.CompilerParams(dimension_semantics=("arbitrary",) * n_axes,
                                vmem_limit_bytes=VMEM_LIMIT)


def _proj_kernel(x_ref, ctab_ref, stab_ref, ln_g_ref, w_in_ref, qn_g_ref, w_q_ref, kvn_g_ref, w_ukv_ref,
                 w_cdft_ref, q_ref, k_ref, kv_ref, vr_ref, vi_ref):
    scale = (QK_NOPE_DIM + QK_ROPE_DIM) ** -0.5
    h = _rms(x_ref[0], ln_g_ref[...]).astype(BF16)
    z = _dot(h, w_in_ref[...])
    ctab = ctab_ref[...]
    stab = stab_ref[...]
    lane = lax.broadcasted_iota(jnp.int32, ctab.shape, 1)
    nope = lane < QK_NOPE_DIM
    qcos = (ctab + nope.astype(F32)) * scale
    qsin = stab * scale

    cq = _rms(z[:, _O_CQ:_O_CQ + Q_LORA_RANK], qn_g_ref[...]).astype(BF16)
    q12 = _dot(cq, w_q_ref[...])
    ckv = _rms(z[:, _O_CKV:_O_CKV + KV_LORA_RANK], kvn_g_ref[...]).astype(BF16)
    kv = _dot(ckv, w_ukv_ref[...])
    kr = z[:, _O_KR:_O_KR + HEAD_PAD] * ctab + z[:, _O_KRR:_O_KRR + HEAD_PAD] * stab
    off = MLA_HEADS * HEAD_PAD
    for hd in range(MLA_HEADS):
        sl = slice(hd * HEAD_PAD, (hd + 1) * HEAD_PAD)
        q_h = q12[:, sl] * qcos + q12[:, off + hd * HEAD_PAD:off + (hd + 1) * HEAD_PAD] * qsin
        q_ref[0, :, sl] = q_h.astype(BF16)
        kv_h = kv[:, sl]
        k_ref[0, :, sl] = jnp.where(nope, kv_h, kr).astype(BF16)
        kv_ref[0, :, sl] = kv_h.astype(BF16)

    u = z[:, _O_U:_O_U + FNET_WIDTH].astype(BF16)
    v = _dot(u, w_cdft_ref[...])
    vr_ref[0] = v[:, :FNET_WIDTH].astype(BF16)
    vi_ref[0] = v[:, FNET_WIDTH:].astype(BF16)


def _proj(x, ctab, stab, ln_g, w_in_ext, qn_g, w_q, kvn_g, w_ukv, w_cdft, tile):
    B, S, _ = x.shape
    hp = MLA_HEADS * HEAD_PAD
    tok = lambda w: pl.BlockSpec((1, tile, w), lambda b, t: (b, t, 0))
    tab = pl.BlockSpec((tile, LANES), lambda b, t: (t, 0))
    return pl.pallas_call(
        _proj_kernel,
        grid=(B, S // tile),
        in_specs=[tok(D_MODEL), tab, tab, _const_spec(ln_g.shape), _const_spec(w_in_ext.shape),
                  _const_spec(qn_g.shape), _const_spec(w_q.shape), _const_spec(kvn_g.shape),
                  _const_spec(w_ukv.shape), _const_spec(w_cdft.shape)],
        out_specs=[tok(hp), tok(hp), tok(hp), tok(FNET_WIDTH), tok(FNET_WIDTH)],
        out_shape=[jax.ShapeDtypeStruct((B, S, hp), BF16)] * 3
                  + [jax.ShapeDtypeStruct((B, S, FNET_WIDTH), BF16)] * 2,
        compiler_params=_params(2),
        name="proj",
    )(x, ctab, stab, ln_g, w_in_ext, qn_g, w_q, kvn_g, w_ukv, w_cdft)


def _attn_kernel(q_ref, k_ref, kv_ref, o_ref):
    outs = []
    for hh in range(2):
        sl = slice(hh * HEAD_PAD, (hh + 1) * HEAD_PAD)
        s = _dot_nt(q_ref[0, :, sl], k_ref[0, :, sl])
        m = jnp.max(s, axis=-1, keepdims=True)
        p = jnp.exp(s - m)
        l = jnp.sum(p, axis=-1, keepdims=True)
        o = _dot(p.astype(BF16), kv_ref[0, :, sl])
        outs.append(o / l)
    lane = lax.broadcasted_iota(jnp.int32, outs[0].shape, 1)
    even = pltpu.roll(outs[0], V_HEAD_DIM, 1)
    o_ref[0] = jnp.where(lane < V_HEAD_DIM, even, outs[1]).astype(BF16)


def _attn(q, k, kv, tile):
    B, S, _ = q.shape
    pair = 2 * HEAD_PAD
    n_pairs = MLA_HEADS // 2
    return pl.pallas_call(
        _attn_kernel,
        grid=(B, n_pairs, S // tile),
        in_specs=[pl.BlockSpec((1, tile, pair), lambda b, j, t: (b, t, j)),
                  pl.BlockSpec((1, S, pair), lambda b, j, t: (b, 0, j)),
                  pl.BlockSpec((1, S, pair), lambda b, j, t: (b, 0, j))],
        out_specs=pl.BlockSpec((1, tile, 2 * V_HEAD_DIM), lambda b, j, t: (b, t, j)),
        out_shape=jax.ShapeDtypeStruct((B, S, MLA_WIDTH), BF16),
        compiler_params=_params(3),
        name="attn",
    )(q, k, kv)


def _dft_kernel(cs_ref, ss_ref, vr_ref, vi_ref, w_f_ref, g_ref):
    f = _dot(cs_ref[...], vr_ref[0]) + _dot(ss_ref[...], vi_ref[0])
    g_ref[0] = _dot(f.astype(BF16), w_f_ref[...]).astype(BF16)


def _dft(cs, ss, vr, vi, w_f):
    B, S, _ = vr.shape
    bat = pl.BlockSpec((1, S, FNET_WIDTH), lambda b: (b, 0, 0))
    return pl.pallas_call(
        _dft_kernel,
        grid=(B,),
        in_specs=[_const_spec(cs.shape), _const_spec(ss.shape), bat, bat, _const_spec(w_f.shape)],
        out_specs=bat,
        out_shape=jax.ShapeDtypeStruct((B, S, FNET_WIDTH), BF16),
        compiler_params=_params(1),
        name="dft",
    )(cs, ss, vr, vi, w_f)


def _post_kernel(x_ref, o_ref, g_ref, mem_ref, w_out_ref, ln_x_g_ref, mem_g_ref, w_xq_ref, w_xkv_ref,
                 w_xo_ref, ln_f_g_ref, w_gu_ref, w_d_ref, fin_g_ref, y_ref, kv_scr, a_scr):
    @pl.when(pl.program_id(1) == 0)
    def _():
        m = _rms(mem_ref[0], mem_g_ref[...]).astype(BF16)
        kv_scr[...] = _dot(m, w_xkv_ref[...]).astype(BF16)

    x1 = (x_ref[0] + _dot(o_ref[0], w_out_ref[:MLA_WIDTH, :])
          + _dot(g_ref[0], w_out_ref[MLA_WIDTH:, :]))

    hq = _rms(x1, ln_x_g_ref[...]).astype(BF16)
    q = (_dot(hq, w_xq_ref[...]) * (XATTN_HEAD_DIM ** -0.5)).astype(BF16)
    heads = []
    for hd in range(XATTN_HEADS):
        sl = slice(hd * XATTN_HEAD_DIM, (hd + 1) * XATTN_HEAD_DIM)
        s = _dot_nt(q[:, sl], kv_scr[:, sl])
        m = jnp.max(s, axis=-1, keepdims=True)
        p = jnp.exp(s - m)
        l = jnp.sum(p, axis=-1, keepdims=True)
        vh = kv_scr[:, D_MODEL + hd * XATTN_HEAD_DIM:D_MODEL + (hd + 1) * XATTN_HEAD_DIM]
        heads.append((_dot(p.astype(BF16), vh) / l).astype(BF16))
    x2 = x1 + _dot(jnp.concatenate(heads, axis=-1), w_xo_ref[...])

    hf = _rms(x2, ln_f_g_ref[...]).astype(BF16)
    for c in range(D_FF // FF_CHUNK):
        sl = slice(c * FF_CHUNK, (c + 1) * FF_CHUNK)
        gate = _dot(hf, w_gu_ref[:, sl])
        up = _dot(hf, w_gu_ref[:, D_FF + c * FF_CHUNK:D_FF + (c + 1) * FF_CHUNK])
        a_scr[:, sl] = (gate * jax.nn.sigmoid(gate) * up).astype(BF16)
    x3 = x2 + _dot(a_scr[...], w_d_ref[...])
    y_ref[0] = _rms(x3, fin_g_ref[...])


def _post(x, o, g, mem, w_out, ln_x_g, mem_g, w_xq, w_xkv, w_xo, ln_f_g, w_gu, w_d, fin_g, tile):
    B, S, _ = x.shape
    M = mem.shape[1]
    tok = lambda w: pl.BlockSpec((1, tile, w), lambda b, t: (b, t, 0))
    consts = [w_out, ln_x_g, mem_g, w_xq, w_xkv, w_xo, ln_f_g, w_gu, w_d, fin_g]
    return pl.pallas_call(
        _post_kernel,
        grid=(B, S // tile),
        in_specs=[tok(D_MODEL), tok(MLA_WIDTH), tok(FNET_WIDTH),
                  pl.BlockSpec((1, M, D_MODEL), lambda b, t: (b, 0, 0))]
                 + [_const_spec(c.shape) for c in consts],
        out_specs=tok(D_MODEL),
        out_shape=jax.ShapeDtypeStruct((B, S, D_MODEL), F32),
        scratch_shapes=[pltpu.VMEM((M, 2 * D_MODEL), BF16), pltpu.VMEM((tile, D_FF), BF16)],
        compiler_params=_params(2),
        name="post",
    )(x, o, g, mem, *consts)


def _rot_cols(w):
    half = QK_ROPE_DIM // 2
    return jnp.concatenate([-w[..., half:], w[..., :half]], axis=-1)


def _pad_rope_lanes(w):
    rows = w.shape[0]
    return jnp.concatenate([jnp.zeros((rows, QK_NOPE_DIM), F32), w,
                            jnp.zeros((rows, HEAD_PAD - QK_NOPE_DIM - QK_ROPE_DIM), F32)], axis=-1)


def _prep_weights(w_in, w_uq, w_fnet):
    w_cq = w_in[:, :Q_LORA_RANK]
    w_ckv = w_in[:, Q_LORA_RANK:Q_LORA_RANK + KV_LORA_RANK]
    o3 = Q_LORA_RANK + KV_LORA_RANK
    w_kr = w_in[:, o3:o3 + QK_ROPE_DIM]
    w_u = w_in[:, o3 + QK_ROPE_DIM:]
    w_in_ext = jnp.concatenate([w_cq, w_ckv, _pad_rope_lanes(w_kr), _pad_rope_lanes(_rot_cols(w_kr)), w_u],
                               axis=-1).astype(BF16)

    wq = w_uq.reshape(Q_LORA_RANK, MLA_HEADS, QK_NOPE_DIM + QK_ROPE_DIM)
    w_nope, w_rope = wq[..., :QK_NOPE_DIM], wq[..., QK_NOPE_DIM:]
    pad = jnp.zeros((Q_LORA_RANK, MLA_HEADS, HEAD_PAD - QK_NOPE_DIM - QK_ROPE_DIM), F32)
    w_plain = jnp.concatenate([w_nope, w_rope, pad], axis=-1).reshape(Q_LORA_RANK, -1)
    w_rot = jnp.concatenate([jnp.zeros_like(w_nope), _rot_cols(w_rope), pad], axis=-1).reshape(Q_LORA_RANK, -1)
    w_q = jnp.concatenate([w_plain, w_rot], axis=-1).astype(BF16)

    w_f = jnp.zeros((FNET_WIDTH, FNET_WIDTH), F32)
    for gi in range(FNET_GROUPS):
        sl = slice(gi * FNET_GROUP_DIM, (gi + 1) * FNET_GROUP_DIM)
        w_f = w_f.at[sl, sl].set(w_fnet[gi])
    return w_in_ext, w_q, w_f.astype(BF16)


def _dft_mats(n):
    idx = jnp.arange(n, dtype=jnp.int32)
    ang = ((idx[:, None] * idx[None, :]) % n).astype(F32) * (2.0 * jnp.pi / n)
    return jnp.cos(ang), jnp.sin(ang)


def _channel_dft():
    c, s = _dft_mats(FNET_GROUP_DIM)
    norm = FNET_GROUP_DIM ** -0.5
    eye = jnp.eye(FNET_GROUPS, dtype=F32)
    return jnp.concatenate([jnp.kron(eye, c * norm), jnp.kron(eye, -s * norm)], axis=-1).astype(BF16)


def _rope_tabs(seq):
    inv = 1.0 / (ROPE_BASE ** (jnp.arange(0, QK_ROPE_DIM, 2, dtype=F32) / QK_ROPE_DIM))
    ang = jnp.arange(seq, dtype=F32)[:, None] * inv[None, :]
    cos2 = jnp.concatenate([jnp.cos(ang), jnp.cos(ang)], axis=-1)
    sin2 = jnp.concatenate([jnp.sin(ang), jnp.sin(ang)], axis=-1)
    return _pad_rope_lanes(cos2), _pad_rope_lanes(sin2)


def _trunk(x, mem, w, tiles):
    S = x.shape[1]
    ctab, stab = _rope_tabs(S)
    cs, ss = _dft_mats(S)
    norm = S ** -0.5
    cs, ss = (cs * norm).astype(BF16), (ss * norm).astype(BF16)
    q, k, kv, vr, vi = _proj(x, ctab, stab, w["ln_mix_g"], w["w_in_ext"], w["q_norm_g"], w["w_q"],
                             w["kv_norm_g"], w["w_ukv"], w["w_cdft"], min(tiles[0], S))
    o = _attn(q, k, kv, min(tiles[1], S))
    g = _dft(cs, ss, vr, vi, w["w_f"])
    return _post(x, o, g, mem, w["w_out"], w["ln_x_g"], w["mem_norm_g"], w["w_xq"], w["w_xkv"], w["w_xo"],
                 w["ln_ffn_g"], w["w_gate_up"], w["w_down"], w["final_norm_g"], min(tiles[2], S))


def kernel(x_prompt, x_sample, mem_prompt, mem_sample, ln_mix_g, w_in, q_norm_g, w_uq, kv_norm_g, w_ukv, w_fnet,
           w_out, ln_x_g, mem_norm_g, w_xq, w_xkv, w_xo, ln_ffn_g, w_gate_up, w_down, final_norm_g):
    assert ln_mix_g.shape[0] == 1, "single-layer trunk"
    w_in_ext, w_q, w_f = _prep_weights(w_in[0], w_uq[0], w_fnet[0])
    row = lambda g: g.reshape(1, -1).astype(F32)
    w = dict(
        ln_mix_g=row(ln_mix_g[0]), w_in_ext=w_in_ext, q_norm_g=row(q_norm_g[0]), w_q=w_q,
        kv_norm_g=row(kv_norm_g[0]), w_ukv=w_ukv[0].astype(BF16), w_cdft=_channel_dft(), w_f=w_f,
        w_out=w_out[0].astype(BF16), ln_x_g=row(ln_x_g[0]), mem_norm_g=row(mem_norm_g[0]),
        w_xq=w_xq[0].astype(BF16), w_xkv=w_xkv[0].astype(BF16), w_xo=w_xo[0].astype(BF16),
        ln_ffn_g=row(ln_ffn_g[0]), w_gate_up=w_gate_up[0].astype(BF16), w_down=w_down[0].astype(BF16),
        final_norm_g=row(final_norm_g),
    )
    return (_trunk(x_prompt, mem_prompt, w, TILES), _trunk(x_sample, mem_sample, w, TILES))
```

```python
import functools

import jax
import jax.numpy as jnp
from jax import lax
from jax.experimental import pallas as pl
from jax.experimental.pallas import tpu as pltpu

F32 = jnp.float32
BF16 = jnp.bfloat16

D_MODEL = 1024
MLA_HEADS = 8
QK_NOPE_DIM = 64
QK_ROPE_DIM = 32
V_HEAD_DIM = 64
Q_LORA_RANK = 384
KV_LORA_RANK = 256
FNET_GROUPS = 4
FNET_GROUP_DIM = 128
FNET_WIDTH = FNET_GROUPS * FNET_GROUP_DIM
MLA_WIDTH = MLA_HEADS * V_HEAD_DIM
XATTN_HEADS = 4
XATTN_HEAD_DIM = D_MODEL // XATTN_HEADS
D_FF = 2816
ROPE_BASE = 10000.0
NORM_EPS = 1e-6
LOG2_E = 1.4426950408889634

LANES = 128
HEAD_PAD = LANES
FF_CHUNK = 256
VMEM_LIMIT = 56 * 1024 * 1024
TILES = (512, 512, 256)
ATTN_HEADS_PER_STEP = 8

_O_CQ = 0
_O_CKV = _O_CQ + Q_LORA_RANK
_O_KR = _O_CKV + KV_LORA_RANK
_O_KRR = _O_KR + HEAD_PAD
_O_U = _O_KRR + HEAD_PAD
IN_EXT = _O_U + FNET_WIDTH


def _rms(x, g):
    return x * lax.rsqrt(jnp.mean(x * x, axis=-1, keepdims=True) + NORM_EPS) * g


def _dot(a, b):
    return jnp.dot(a, b, preferred_element_type=F32)


def _dot_nt(a, b):
    return lax.dot_general(a, b, (((1,), (1,)), ((), ())), preferred_element_type=F32)


def _const_spec(shape):
    zeros = (0,) * len(shape)
    return pl.BlockSpec(shape, lambda *_: zeros, pipeline_mode=pl.Buffered(1))


def _params(n_axes):
    return pltpu.CompilerParams(dimension_semantics=("arbitrary",) * n_axes,
                                vmem_limit_bytes=VMEM_LIMIT)


def _proj_kernel(x_ref, ctab_ref, stab_ref, ctab_t_ref, stab_t_ref, ln_g_ref, w_in_ref, qn_g_ref, w_q_t_ref,
                 kvn_g_ref, w_ukv_ref, w_uv_t_ref, w_cdft_ref, q_t_ref, k_ref, v_t_ref, vr_ref, vi_ref):
    scale = (QK_NOPE_DIM + QK_ROPE_DIM) ** -0.5 * LOG2_E
    h = _rms(x_ref[0], ln_g_ref[...]).astype(BF16)
    z = _dot(h, w_in_ref[...])

    cq = _rms(z[:, _O_CQ:_O_CQ + Q_LORA_RANK], qn_g_ref[...]).astype(BF16)
    q12_t = _dot_nt(w_q_t_ref[...], cq)
    ctab_t = ctab_t_ref[...]
    row = lax.broadcasted_iota(jnp.int32, ctab_t.shape, 0)
    qcos_t = (ctab_t + (row < QK_NOPE_DIM).astype(F32)) * scale
    qsin_t = stab_t_ref[...] * scale
    off = MLA_HEADS * HEAD_PAD
    for hd in range(MLA_HEADS):
        sl = slice(hd * HEAD_PAD, (hd + 1) * HEAD_PAD)
        q_h = q12_t[sl, :] * qcos_t + q12_t[off + hd * HEAD_PAD:off + (hd + 1) * HEAD_PAD, :] * qsin_t
        q_t_ref[0, sl, :] = q_h.astype(BF16)

    ckv = _rms(z[:, _O_CKV:_O_CKV + KV_LORA_RANK], kvn_g_ref[...]).astype(BF16)
    kv = _dot(ckv, w_ukv_ref[...])
    ctab = ctab_ref[...]
    kr = z[:, _O_KR:_O_KR + HEAD_PAD] * ctab + z[:, _O_KRR:_O_KRR + HEAD_PAD] * stab_ref[...]
    nope = lax.broadcasted_iota(jnp.int32, ctab.shape, 1) < QK_NOPE_DIM
    for hd in range(MLA_HEADS):
        sl = slice(hd * HEAD_PAD, (hd + 1) * HEAD_PAD)
        k_ref[0, :, sl] = jnp.where(nope, kv[:, sl], kr).astype(BF16)
    v_t_ref[0] = _dot_nt(w_uv_t_ref[...], ckv).astype(BF16)

    u = z[:, _O_U:_O_U + FNET_WIDTH].astype(BF16)
    v = _dot(u, w_cdft_ref[...])
    vr_ref[0] = v[:, :FNET_WIDTH].astype(BF16)
    vi_ref[0] = v[:, FNET_WIDTH:].astype(BF16)


def _proj(x, ctab, stab, ln_g, w_in_ext, qn_g, w_q_t, kvn_g, w_ukv, w_uv_t, w_cdft, tile):
    B, S, _ = x.shape
    hp = MLA_HEADS * HEAD_PAD
    tok = lambda w: pl.BlockSpec((1, tile, w), lambda b, t: (b, t, 0))
    tok_t = lambda w: pl.BlockSpec((1, w, tile), lambda b, t: (b, 0, t))
    tab = pl.BlockSpec((tile, LANES), lambda b, t: (t, 0))
    tab_t = pl.BlockSpec((LANES, tile), lambda b, t: (0, t))
    return pl.pallas_call(
        _proj_kernel,
        grid=(B, S // tile),
        in_specs=[tok(D_MODEL), tab, tab, tab_t, tab_t, _const_spec(ln_g.shape), _const_spec(w_in_ext.shape),
                  _const_spec(qn_g.shape), _const_spec(w_q_t.shape), _const_spec(kvn_g.shape),
                  _const_spec(w_ukv.shape), _const_spec(w_uv_t.shape), _const_spec(w_cdft.shape)],
        out_specs=[tok_t(hp), tok(hp), tok_t(MLA_WIDTH), tok(FNET_WIDTH), tok(FNET_WIDTH)],
        out_shape=[jax.ShapeDtypeStruct((B, hp, S), BF16), jax.ShapeDtypeStruct((B, S, hp), BF16),
                   jax.ShapeDtypeStruct((B, MLA_WIDTH, S), BF16)]
                  + [jax.ShapeDtypeStruct((B, S, FNET_WIDTH), BF16)] * 2,
        compiler_params=_params(2),
        name="proj",
    )(x, ctab, stab, ctab.T, stab.T, ln_g, w_in_ext, qn_g, w_q_t, kvn_g, w_ukv, w_uv_t, w_cdft)


def _attn_kernel(q_t_ref, k_ref, v_t_ref, o_ref):
    def scores(hh):
        return _dot(k_ref[0, :, hh * HEAD_PAD:(hh + 1) * HEAD_PAD],
                    q_t_ref[0, hh * HEAD_PAD:(hh + 1) * HEAD_PAD, :])

    outs = []
    s_next = scores(0)
    for hh in range(ATTN_HEADS_PER_STEP):
        s = s_next
        if hh + 1 < ATTN_HEADS_PER_STEP:
            s_next = scores(hh + 1)
        m = jnp.max(s, axis=0, keepdims=True)
        p = jnp.exp2(s - m)
        l = jnp.sum(p, axis=0, keepdims=True)
        o_t = _dot(v_t_ref[0, hh * V_HEAD_DIM:(hh + 1) * V_HEAD_DIM, :], p.astype(BF16))
        outs.append(o_t / l)
    o_ref[0] = jnp.concatenate(outs, axis=0).T.astype(BF16)


def _attn(q_t, k, v_t, tile):
    B, S, _ = k.shape
    hps = ATTN_HEADS_PER_STEP
    return pl.pallas_call(
        _attn_kernel,
        grid=(B, MLA_HEADS // hps, S // tile),
        in_specs=[pl.BlockSpec((1, hps * HEAD_PAD, tile), lambda b, j, t: (b, j, t)),
                  pl.BlockSpec((1, S, hps * HEAD_PAD), lambda b, j, t: (b, 0, j)),
                  pl.BlockSpec((1, hps * V_HEAD_DIM, S), lambda b, j, t: (b, j, 0))],
        out_specs=pl.BlockSpec((1, tile, hps * V_HEAD_DIM), lambda b, j, t: (b, t, j)),
        out_shape=jax.ShapeDtypeStruct((B, S, MLA_WIDTH), BF16),
        compiler_params=_params(3),
        name="attn",
    )(q_t, k, v_t)


def _dft_kernel(cs_ref, ss_ref, vr_ref, vi_ref, w_f_ref, g_ref):
    f = _dot(cs_ref[...], vr_ref[0]) + _dot(ss_ref[...], vi_ref[0])
    g_ref[0] = _dot(f.astype(BF16), w_f_ref[...]).astype(BF16)


def _dft(cs, ss, vr, vi, w_f):
    B, S, _ = vr.shape
    bat = pl.BlockSpec((1, S, FNET_WIDTH), lambda b: (b, 0, 0))
    return pl.pallas_call(
        _dft_kernel,
        grid=(B,),
        in_specs=[_const_spec(cs.shape), _const_spec(ss.shape), bat, bat, _const_spec(w_f.shape)],
        out_specs=bat,
        out_shape=jax.ShapeDtypeStruct((B, S, FNET_WIDTH), BF16),
        compiler_params=_params(1),
        name="dft",
    )(cs, ss, vr, vi, w_f)


def _post_kernel(x_ref, o_ref, g_ref, mem_ref, w_out_ref, ln_x_g_ref, mem_g_ref, w_xq_ref, w_xkv_ref,
                 w_xo_ref, ln_f_g_ref, w_gu_ref, w_d_ref, fin_g_ref, y_ref, kv_scr, a_scr):
    @pl.when(pl.program_id(1) == 0)
    def _():
        m = _rms(mem_ref[0], mem_g_ref[...]).astype(BF16)
        kv_scr[...] = _dot(m, w_xkv_ref[...]).astype(BF16)

    x1 = (x_ref[0] + _dot(o_ref[0], w_out_ref[:MLA_WIDTH, :])
          + _dot(g_ref[0], w_out_ref[MLA_WIDTH:, :]))

    hq = _rms(x1, ln_x_g_ref[...]).astype(BF16)
    q = (_dot(hq, w_xq_ref[...]) * (XATTN_HEAD_DIM ** -0.5)).astype(BF16)
    heads = []
    for hd in range(XATTN_HEADS):
        sl = slice(hd * XATTN_HEAD_DIM, (hd + 1) * XATTN_HEAD_DIM)
        s = _dot_nt(q[:, sl], kv_scr[:, sl])
        m = jnp.max(s, axis=-1, keepdims=True)
        p = jnp.exp(s - m)
        l = jnp.sum(p, axis=-1, keepdims=True)
        vh = kv_scr[:, D_MODEL + hd * XATTN_HEAD_DIM:D_MODEL + (hd + 1) * XATTN_HEAD_DIM]
        heads.append((_dot(p.astype(BF16), vh) / l).astype(BF16))
    x2 = x1 + _dot(jnp.concatenate(heads, axis=-1), w_xo_ref[...])

    hf = _rms(x2, ln_f_g_ref[...]).astype(BF16)
    for c in range(D_FF // FF_CHUNK):
        sl = slice(c * FF_CHUNK, (c + 1) * FF_CHUNK)
        gate = _dot(hf, w_gu_ref[:, sl])
        up = _dot(hf, w_gu_ref[:, D_FF + c * FF_CHUNK:D_FF + (c + 1) * FF_CHUNK])
        a_scr[:, sl] = (gate * jax.nn.sigmoid(gate) * up).astype(BF16)
    x3 = x2 + _dot(a_scr[...], w_d_ref[...])
    y_ref[0] = _rms(x3, fin_g_ref[...])


def _post(x, o, g, mem, w_out, ln_x_g, mem_g, w_xq, w_xkv, w_xo, ln_f_g, w_gu, w_d, fin_g, tile):
    B, S, _ = x.shape
    M = mem.shape[1]
    tok = lambda w: pl.BlockSpec((1, tile, w), lambda b, t: (b, t, 0))
    consts = [w_out, ln_x_g, mem_g, w_xq, w_xkv, w_xo, ln_f_g, w_gu, w_d, fin_g]
    return pl.pallas_call(
        _post_kernel,
        grid=(B, S // tile),
        in_specs=[tok(D_MODEL), tok(MLA_WIDTH), tok(FNET_WIDTH),
                  pl.BlockSpec((1, M, D_MODEL), lambda b, t: (b, 0, 0))]
                 + [_const_spec(c.shape) for c in consts],
        out_specs=tok(D_MODEL),
        out_shape=jax.ShapeDtypeStruct((B, S, D_MODEL), F32),
        scratch_shapes=[pltpu.VMEM((M, 2 * D_MODEL), BF16), pltpu.VMEM((tile, D_FF), BF16)],
        compiler_params=_params(2),
        name="post",
    )(x, o, g, mem, *consts)


def _rot_cols(w):
    half = QK_ROPE_DIM // 2
    return jnp.concatenate([-w[..., half:], w[..., :half]], axis=-1)


def _pad_rope_lanes(w):
    rows = w.shape[0]
    return jnp.concatenate([jnp.zeros((rows, QK_NOPE_DIM), F32), w,
                            jnp.zeros((rows, HEAD_PAD - QK_NOPE_DIM - QK_ROPE_DIM), F32)], axis=-1)


def _prep_weights(w_in, w_uq, w_ukv, w_fnet):
    w_cq = w_in[:, :Q_LORA_RANK]
    w_ckv = w_in[:, Q_LORA_RANK:Q_LORA_RANK + KV_LORA_RANK]
    o3 = Q_LORA_RANK + KV_LORA_RANK
    w_kr = w_in[:, o3:o3 + QK_ROPE_DIM]
    w_u = w_in[:, o3 + QK_ROPE_DIM:]
    w_in_ext = jnp.concatenate([w_cq, w_ckv, _pad_rope_lanes(w_kr), _pad_rope_lanes(_rot_cols(w_kr)), w_u],
                               axis=-1).astype(BF16)

    wq = w_uq.reshape(Q_LORA_RANK, MLA_HEADS, QK_NOPE_DIM + QK_ROPE_DIM)
    w_nope, w_rope = wq[..., :QK_NOPE_DIM], wq[..., QK_NOPE_DIM:]
    pad = jnp.zeros((Q_LORA_RANK, MLA_HEADS, HEAD_PAD - QK_NOPE_DIM - QK_ROPE_DIM), F32)
    w_plain = jnp.concatenate([w_nope, w_rope, pad], axis=-1).reshape(Q_LORA_RANK, -1)
    w_rot = jnp.concatenate([jnp.zeros_like(w_nope), _rot_cols(w_rope), pad], axis=-1).reshape(Q_LORA_RANK, -1)
    w_q_t = jnp.concatenate([w_plain, w_rot], axis=-1).T.astype(BF16)

    kv_w = w_ukv.reshape(KV_LORA_RANK, MLA_HEADS, QK_NOPE_DIM + V_HEAD_DIM)
    w_uv_t = kv_w[..., QK_NOPE_DIM:].reshape(KV_LORA_RANK, MLA_WIDTH).T.astype(BF16)

    w_f = jnp.zeros((FNET_WIDTH, FNET_WIDTH), F32)
    for gi in range(FNET_GROUPS):
        sl = slice(gi * FNET_GROUP_DIM, (gi + 1) * FNET_GROUP_DIM)
        w_f = w_f.at[sl, sl].set(w_fnet[gi])
    return w_in_ext, w_q_t, w_uv_t, w_f.astype(BF16)


def _dft_mats(n):
    idx = jnp.arange(n, dtype=jnp.int32)
    ang = ((idx[:, None] * idx[None, :]) % n).astype(F32) * (2.0 * jnp.pi / n)
    return jnp.cos(ang), jnp.sin(ang)


def _channel_dft():
    c, s = _dft_mats(FNET_GROUP_DIM)
    norm = FNET_GROUP_DIM ** -0.5
    eye = jnp.eye(FNET_GROUPS, dtype=F32)
    return jnp.concatenate([jnp.kron(eye, c * norm), jnp.kron(eye, -s * norm)], axis=-1).astype(BF16)


def _rope_tabs(seq):
    inv = 1.0 / (ROPE_BASE ** (jnp.arange(0, QK_ROPE_DIM, 2, dtype=F32) / QK_ROPE_DIM))
    ang = jnp.arange(seq, dtype=F32)[:, None] * inv[None, :]
    cos2 = jnp.concatenate([jnp.cos(ang), jnp.cos(ang)], axis=-1)
    sin2 = jnp.concatenate([jnp.sin(ang), jnp.sin(ang)], axis=-1)
    return _pad_rope_lanes(cos2), _pad_rope_lanes(sin2)


def _trunk(x, mem, w, tiles):
    S = x.shape[1]
    ctab, stab = _rope_tabs(S)
    cs, ss = _dft_mats(S)
    norm = S ** -0.5
    cs, ss = (cs * norm).astype(BF16), (ss * norm).astype(BF16)
    q_t, k, v_t, vr, vi = _proj(x, ctab, stab, w["ln_mix_g"], w["w_in_ext"], w["q_norm_g"], w["w_q_t"],
                                w["kv_norm_g"], w["w_ukv"], w["w_uv_t"], w["w_cdft"], min(tiles[0], S))
    o = _attn(q_t, k, v_t, min(tiles[1], S))
    g = _dft(cs, ss, vr, vi, w["w_f"])
    return _post(x, o, g, mem, w["w_out"], w["ln_x_g"], w["mem_norm_g"], w["w_xq"], w["w_xkv"], w["w_xo"],
                 w["ln_ffn_g"], w["w_gate_up"], w["w_down"], w["final_norm_g"], min(tiles[2], S))


def kernel(x_prompt, x_sample, mem_prompt, mem_sample, ln_mix_g, w_in, q_norm_g, w_uq, kv_norm_g, w_ukv, w_fnet,
           w_out, ln_x_g, mem_norm_g, w_xq, w_xkv, w_xo, ln_ffn_g, w_gate_up, w_down, final_norm_g):
    assert ln_mix_g.shape[0] == 1, "single-layer trunk"
    w_in_ext, w_q_t, w_uv_t, w_f = _prep_weights(w_in[0], w_uq[0], w_ukv[0], w_fnet[0])
    row = lambda g: g.reshape(1, -1).astype(F32)
    w = dict(
        ln_mix_g=row(ln_mix_g[0]), w_in_ext=w_in_ext, q_norm_g=row(q_norm_g[0]), w_q_t=w_q_t,
        kv_norm_g=row(kv_norm_g[0]), w_ukv=w_ukv[0].astype(BF16), w_uv_t=w_uv_t, w_cdft=_channel_dft(), w_f=w_f,
        w_out=w_out[0].astype(BF16), ln_x_g=row(ln_x_g[0]), mem_norm_g=row(mem_norm_g[0]),
        w_xq=w_xq[0].astype(BF16), w_xkv=w_xkv[0].astype(BF16), w_xo=w_xo[0].astype(BF16),
        ln_ffn_g=row(ln_ffn_g[0]), w_gate_up=w_gate_up[0].astype(BF16), w_down=w_down[0].astype(BF16),
        final_norm_g=row(final_norm_g),
    )
    return (_trunk(x_prompt, mem_prompt, w, TILES), _trunk(x_sample, mem_sample, w, TILES))
```

```python
import functools

import jax
import jax.numpy as jnp
from jax import lax
from jax.experimental import pallas as pl
from jax.experimental.pallas import tpu as pltpu

F32 = jnp.float32
BF16 = jnp.bfloat16

D_MODEL = 1024
MLA_HEADS = 8
QK_NOPE_DIM = 64
QK_ROPE_DIM = 32
V_HEAD_DIM = 64
Q_LORA_RANK = 384
KV_LORA_RANK = 256
FNET_GROUPS = 4
FNET_GROUP_DIM = 128
FNET_WIDTH = FNET_GROUPS * FNET_GROUP_DIM
MLA_WIDTH = MLA_HEADS * V_HEAD_DIM
XATTN_HEADS = 4
XATTN_HEAD_DIM = D_MODEL // XATTN_HEADS
D_FF = 2816
ROPE_BASE = 10000.0
NORM_EPS = 1e-6
LOG2_E = 1.4426950408889634

LANES = 128
HEAD_PAD = LANES
FF_CHUNK = 256
VMEM_LIMIT = 56 * 1024 * 1024
TILES = (512, 512, 256)
ATTN_HEADS_PER_STEP = 8
DFT_LEVELS = 3
DFT_ROW_CHUNK = 256

_O_CQ = 0
_O_CKV = _O_CQ + Q_LORA_RANK
_O_KR = _O_CKV + KV_LORA_RANK
_O_KRR = _O_KR + HEAD_PAD
_O_U = _O_KRR + HEAD_PAD
IN_EXT = _O_U + FNET_WIDTH


def _rms(x, g):
    return x * lax.rsqrt(jnp.mean(x * x, axis=-1, keepdims=True) + NORM_EPS) * g


def _dot(a, b):
    return jnp.dot(a, b, preferred_element_type=F32)


def _dot_nt(a, b):
    return lax.dot_general(a, b, (((1,), (1,)), ((), ())), preferred_element_type=F32)


def _const_spec(shape):
    zeros = (0,) * len(shape)
    return pl.BlockSpec(shape, lambda *_: zeros, pipeline_mode=pl.Buffered(1))


def _params(n_axes):
    return pltpu.CompilerParams(dimension_semantics=("arbitrary",) * n_axes,
                                vmem_limit_bytes=VMEM_LIMIT)


def _proj_kernel(x_ref, ctab_ref, stab_ref, ctab_t_ref, stab_t_ref, ln_g_ref, w_in_ref, qn_g_ref, w_q_t_ref,
                 kvn_g_ref, w_ukv_ref, w_uv_t_ref, w_cdft_ref, q_t_ref, k_ref, v_t_ref, vr_ref, vi_ref):
    scale = (QK_NOPE_DIM + QK_ROPE_DIM) ** -0.5 * LOG2_E
    h = _rms(x_ref[0], ln_g_ref[...]).astype(BF16)
    z = _dot(h, w_in_ref[...])

    cq = _rms(z[:, _O_CQ:_O_CQ + Q_LORA_RANK], qn_g_ref[...]).astype(BF16)
    q12_t = _dot_nt(w_q_t_ref[...], cq)
    ctab_t = ctab_t_ref[...]
    row = lax.broadcasted_iota(jnp.int32, ctab_t.shape, 0)
    qcos_t = (ctab_t + (row < QK_NOPE_DIM).astype(F32)) * scale
    qsin_t = stab_t_ref[...] * scale
    off = MLA_HEADS * HEAD_PAD
    for hd in range(MLA_HEADS):
        sl = slice(hd * HEAD_PAD, (hd + 1) * HEAD_PAD)
        q_h = q12_t[sl, :] * qcos_t + q12_t[off + hd * HEAD_PAD:off + (hd + 1) * HEAD_PAD, :] * qsin_t
        q_t_ref[0, sl, :] = q_h.astype(BF16)

    ckv = _rms(z[:, _O_CKV:_O_CKV + KV_LORA_RANK], kvn_g_ref[...]).astype(BF16)
    kv = _dot(ckv, w_ukv_ref[...])
    ctab = ctab_ref[...]
    kr = z[:, _O_KR:_O_KR + HEAD_PAD] * ctab + z[:, _O_KRR:_O_KRR + HEAD_PAD] * stab_ref[...]
    nope = lax.broadcasted_iota(jnp.int32, ctab.shape, 1) < QK_NOPE_DIM
    for hd in range(MLA_HEADS):
        sl = slice(hd * HEAD_PAD, (hd + 1) * HEAD_PAD)
        k_ref[0, :, sl] = jnp.where(nope, kv[:, sl], kr).astype(BF16)
    v_t_ref[0] = _dot_nt(w_uv_t_ref[...], ckv).astype(BF16)

    u = z[:, _O_U:_O_U + FNET_WIDTH].astype(BF16)
    v = _dot(u, w_cdft_ref[...])
    vr_ref[0] = v[:, :FNET_WIDTH].astype(BF16)
    vi_ref[0] = v[:, FNET_WIDTH:].astype(BF16)


def _proj(x, ctab, stab, ln_g, w_in_ext, qn_g, w_q_t, kvn_g, w_ukv, w_uv_t, w_cdft, tile):
    B, S, _ = x.shape
    hp = MLA_HEADS * HEAD_PAD
    tok = lambda w: pl.BlockSpec((1, tile, w), lambda b, t: (b, t, 0))
    tok_t = lambda w: pl.BlockSpec((1, w, tile), lambda b, t: (b, 0, t))
    tab = pl.BlockSpec((tile, LANES), lambda b, t: (t, 0))
    tab_t = pl.BlockSpec((LANES, tile), lambda b, t: (0, t))
    return pl.pallas_call(
        _proj_kernel,
        grid=(B, S // tile),
        in_specs=[tok(D_MODEL), tab, tab, tab_t, tab_t, _const_spec(ln_g.shape), _const_spec(w_in_ext.shape),
                  _const_spec(qn_g.shape), _const_spec(w_q_t.shape), _const_spec(kvn_g.shape),
                  _const_spec(w_ukv.shape), _const_spec(w_uv_t.shape), _const_spec(w_cdft.shape)],
        out_specs=[tok_t(hp), tok(hp), tok_t(MLA_WIDTH), tok(FNET_WIDTH), tok(FNET_WIDTH)],
        out_shape=[jax.ShapeDtypeStruct((B, hp, S), BF16), jax.ShapeDtypeStruct((B, S, hp), BF16),
                   jax.ShapeDtypeStruct((B, MLA_WIDTH, S), BF16)]
                  + [jax.ShapeDtypeStruct((B, S, FNET_WIDTH), BF16)] * 2,
        compiler_params=_params(2),
        name="proj",
    )(x, ctab, stab, ctab.T, stab.T, ln_g, w_in_ext, qn_g, w_q_t, kvn_g, w_ukv, w_uv_t, w_cdft)


def _attn_kernel(q_t_ref, k_ref, v_t_ref, o_ref):
    def scores(hh):
        return _dot(k_ref[0, :, hh * HEAD_PAD:(hh + 1) * HEAD_PAD],
                    q_t_ref[0, hh * HEAD_PAD:(hh + 1) * HEAD_PAD, :])

    outs = []
    s_next = scores(0)
    for hh in range(ATTN_HEADS_PER_STEP):
        s = s_next
        if hh + 1 < ATTN_HEADS_PER_STEP:
            s_next = scores(hh + 1)
        m = jnp.max(s, axis=0, keepdims=True)
        p = jnp.exp2(s - m)
        l = jnp.sum(p, axis=0, keepdims=True)
        o_t = _dot(v_t_ref[0, hh * V_HEAD_DIM:(hh + 1) * V_HEAD_DIM, :], p.astype(BF16))
        outs.append(o_t / l)
    o_ref[0] = jnp.concatenate(outs, axis=0).T.astype(BF16)


def _attn(q_t, k, v_t, tile):
    B, S, _ = k.shape
    hps = ATTN_HEADS_PER_STEP
    return pl.pallas_call(
        _attn_kernel,
        grid=(B, MLA_HEADS // hps, S // tile),
        in_specs=[pl.BlockSpec((1, hps * HEAD_PAD, tile), lambda b, j, t: (b, j, t)),
                  pl.BlockSpec((1, S, hps * HEAD_PAD), lambda b, j, t: (b, 0, j)),
                  pl.BlockSpec((1, hps * V_HEAD_DIM, S), lambda b, j, t: (b, j, 0))],
        out_specs=pl.BlockSpec((1, tile, hps * V_HEAD_DIM), lambda b, j, t: (b, t, j)),
        out_shape=jax.ShapeDtypeStruct((B, S, MLA_WIDTH), BF16),
        compiler_params=_params(3),
        name="attn",
    )(q_t, k, v_t)


def _bit_reverse(i, bits):
    return int(format(i, "0%db" % bits)[::-1], 2)


def _dft_kernel(vr_ref, vi_ref, twc_ref, tws_ref, cs_ref, w_f_ref, g_ref, xr_scr, xi_scr, xb_scr):
    S = vr_ref.shape[1]
    n_leaves = 2 ** DFT_LEVELS
    leaf = S // n_leaves
    chunk = min(DFT_ROW_CHUNK, leaf)
    reps = FNET_WIDTH // LANES
    n, off = S, 0
    for lev in range(DFT_LEVELS):
        half = n // 2
        last = lev == DFT_LEVELS - 1
        for base in range(0, S, n):
            for c0 in range(0, half, chunk):
                top = slice(base + c0, base + c0 + chunk)
                bot = slice(base + half + c0, base + half + c0 + chunk)
                if lev == 0:
                    tr, ti = vr_ref[0, top, :].astype(F32), vi_ref[0, top, :].astype(F32)
                    br, bi = vr_ref[0, bot, :].astype(F32), vi_ref[0, bot, :].astype(F32)
                else:
                    tr, ti, br, bi = xr_scr[top, :], xi_scr[top, :], xr_scr[bot, :], xi_scr[bot, :]
                c = jnp.concatenate([twc_ref[off + c0:off + c0 + chunk, :]] * reps, axis=1)
                s = jnp.concatenate([tws_ref[off + c0:off + c0 + chunk, :]] * reps, axis=1)
                dr, di = tr - br, ti - bi
                ar, ai = tr + br, ti + bi
                mr, mi = dr * c + di * s, di * c - dr * s
                if last:
                    blk, r0 = (base // n) * 2, c0
                    xb_scr[blk, r0:r0 + chunk, :] = ar.astype(BF16)
                    xb_scr[blk, leaf + r0:leaf + r0 + chunk, :] = ai.astype(BF16)
                    xb_scr[blk + 1, r0:r0 + chunk, :] = mr.astype(BF16)
                    xb_scr[blk + 1, leaf + r0:leaf + r0 + chunk, :] = mi.astype(BF16)
                else:
                    xr_scr[top, :], xi_scr[top, :] = ar, ai
                    xr_scr[bot, :], xi_scr[bot, :] = mr, mi
        off += half
        n = half
    for blk in range(n_leaves):
        f = _dot(cs_ref[...], xb_scr[blk])
        r = _bit_reverse(blk, DFT_LEVELS)
        g_ref[0, :, r * FNET_WIDTH:(r + 1) * FNET_WIDTH] = _dot(f.astype(BF16), w_f_ref[...]).astype(BF16)


def _dft(vr, vi, twc, tws, cs, w_f):
    B, S, _ = vr.shape
    n_leaves = 2 ** DFT_LEVELS
    leaf = S // n_leaves
    bat = pl.BlockSpec((1, S, FNET_WIDTH), lambda b: (b, 0, 0))
    g = pl.pallas_call(
        _dft_kernel,
        grid=(B,),
        in_specs=[bat, bat, _const_spec(twc.shape), _const_spec(tws.shape), _const_spec(cs.shape),
                  _const_spec(w_f.shape)],
        out_specs=pl.BlockSpec((1, leaf, n_leaves * FNET_WIDTH), lambda b: (b, 0, 0)),
        out_shape=jax.ShapeDtypeStruct((B, leaf, n_leaves * FNET_WIDTH), BF16),
        scratch_shapes=[pltpu.VMEM((S, FNET_WIDTH), F32), pltpu.VMEM((S, FNET_WIDTH), F32),
                        pltpu.VMEM((n_leaves, 2 * leaf, FNET_WIDTH), BF16)],
        compiler_params=_params(1),
        name="dft",
    )(vr, vi, twc, tws, cs, w_f)
    return g.reshape(B, S, FNET_WIDTH)


def _post_kernel(x_ref, o_ref, g_ref, mem_ref, w_out_ref, ln_x_g_ref, mem_g_ref, w_xq_ref, w_xkv_ref,
                 w_xo_ref, ln_f_g_ref, w_gu_ref, w_d_ref, fin_g_ref, y_ref, kv_scr, a_scr):
    @pl.when(pl.program_id(1) == 0)
    def _():
        m = _rms(mem_ref[0], mem_g_ref[...]).astype(BF16)
        kv_scr[...] = _dot(m, w_xkv_ref[...]).astype(BF16)

    x1 = (x_ref[0] + _dot(o_ref[0], w_out_ref[:MLA_WIDTH, :])
          + _dot(g_ref[0], w_out_ref[MLA_WIDTH:, :]))

    hq = _rms(x1, ln_x_g_ref[...]).astype(BF16)
    q = (_dot(hq, w_xq_ref[...]) * (XATTN_HEAD_DIM ** -0.5)).astype(BF16)
    heads = []
    for hd in range(XATTN_HEADS):
        sl = slice(hd * XATTN_HEAD_DIM, (hd + 1) * XATTN_HEAD_DIM)
        s = _dot_nt(q[:, sl], kv_scr[:, sl])
        m = jnp.max(s, axis=-1, keepdims=True)
        p = jnp.exp(s - m)
        l = jnp.sum(p, axis=-1, keepdims=True)
        vh = kv_scr[:, D_MODEL + hd * XATTN_HEAD_DIM:D_MODEL + (hd + 1) * XATTN_HEAD_DIM]
        heads.append((_dot(p.astype(BF16), vh) / l).astype(BF16))
    x2 = x1 + _dot(jnp.concatenate(heads, axis=-1), w_xo_ref[...])

    hf = _rms(x2, ln_f_g_ref[...]).astype(BF16)
    for c in range(D_FF // FF_CHUNK):
        sl = slice(c * FF_CHUNK, (c + 1) * FF_CHUNK)
        gate = _dot(hf, w_gu_ref[:, sl])
        up = _dot(hf, w_gu_ref[:, D_FF + c * FF_CHUNK:D_FF + (c + 1) * FF_CHUNK])
        a_scr[:, sl] = (gate * jax.nn.sigmoid(gate) * up).astype(BF16)
    x3 = x2 + _dot(a_scr[...], w_d_ref[...])
    y_ref[0] = _rms(x3, fin_g_ref[...])


def _post(x, o, g, mem, w_out, ln_x_g, mem_g, w_xq, w_xkv, w_xo, ln_f_g, w_gu, w_d, fin_g, tile):
    B, S, _ = x.shape
    M = mem.shape[1]
    tok = lambda w: pl.BlockSpec((1, tile, w), lambda b, t: (b, t, 0))
    consts = [w_out, ln_x_g, mem_g, w_xq, w_xkv, w_xo, ln_f_g, w_gu, w_d, fin_g]
    return pl.pallas_call(
        _post_kernel,
        grid=(B, S // tile),
        in_specs=[tok(D_MODEL), tok(MLA_WIDTH), tok(FNET_WIDTH),
                  pl.BlockSpec((1, M, D_MODEL), lambda b, t: (b, 0, 0))]
                 + [_const_spec(c.shape) for c in consts],
        out_specs=tok(D_MODEL),
        out_shape=jax.ShapeDtypeStruct((B, S, D_MODEL), F32),
        scratch_shapes=[pltpu.VMEM((M, 2 * D_MODEL), BF16), pltpu.VMEM((tile, D_FF), BF16)],
        compiler_params=_params(2),
        name="post",
    )(x, o, g, mem, *consts)


def _rot_cols(w):
    half = QK_ROPE_DIM // 2
    return jnp.concatenate([-w[..., half:], w[..., :half]], axis=-1)


def _pad_rope_lanes(w):
    rows = w.shape[0]
    return jnp.concatenate([jnp.zeros((rows, QK_NOPE_DIM), F32), w,
                            jnp.zeros((rows, HEAD_PAD - QK_NOPE_DIM - QK_ROPE_DIM), F32)], axis=-1)


def _prep_weights(w_in, w_uq, w_ukv, w_fnet):
    w_cq = w_in[:, :Q_LORA_RANK]
    w_ckv = w_in[:, Q_LORA_RANK:Q_LORA_RANK + KV_LORA_RANK]
    o3 = Q_LORA_RANK + KV_LORA_RANK
    w_kr = w_in[:, o3:o3 + QK_ROPE_DIM]
    w_u = w_in[:, o3 + QK_ROPE_DIM:]
    w_in_ext = jnp.concatenate([w_cq, w_ckv, _pad_rope_lanes(w_kr), _pad_rope_lanes(_rot_cols(w_kr)), w_u],
                               axis=-1).astype(BF16)

    wq = w_uq.reshape(Q_LORA_RANK, MLA_HEADS, QK_NOPE_DIM + QK_ROPE_DIM)
    w_nope, w_rope = wq[..., :QK_NOPE_DIM], wq[..., QK_NOPE_DIM:]
    pad = jnp.zeros((Q_LORA_RANK, MLA_HEADS, HEAD_PAD - QK_NOPE_DIM - QK_ROPE_DIM), F32)
    w_plain = jnp.concatenate([w_nope, w_rope, pad], axis=-1).reshape(Q_LORA_RANK, -1)
    w_rot = jnp.concatenate([jnp.zeros_like(w_nope), _rot_cols(w_rope), pad], axis=-1).reshape(Q_LORA_RANK, -1)
    w_q_t = jnp.concatenate([w_plain, w_rot], axis=-1).T.astype(BF16)

    kv_w = w_ukv.reshape(KV_LORA_RANK, MLA_HEADS, QK_NOPE_DIM + V_HEAD_DIM)
    w_uv_t = kv_w[..., QK_NOPE_DIM:].reshape(KV_LORA_RANK, MLA_WIDTH).T.astype(BF16)

    w_f = jnp.zeros((FNET_WIDTH, FNET_WIDTH), F32)
    for gi in range(FNET_GROUPS):
        sl = slice(gi * FNET_GROUP_DIM, (gi + 1) * FNET_GROUP_DIM)
        w_f = w_f.at[sl, sl].set(w_fnet[gi])
    return w_in_ext, w_q_t, w_uv_t, w_f.astype(BF16)


def _dft_mats(n):
    idx = jnp.arange(n, dtype=jnp.int32)
    ang = ((idx[:, None] * idx[None, :]) % n).astype(F32) * (2.0 * jnp.pi / n)
    return jnp.cos(ang), jnp.sin(ang)


def _seq_dft_tables(seq):
    cs_rows, sn_rows = [], []
    n = seq
    for _ in range(DFT_LEVELS):
        ang = jnp.arange(n // 2, dtype=F32) * (2.0 * jnp.pi / n)
        cs_rows.append(jnp.cos(ang))
        sn_rows.append(jnp.sin(ang))
        n //= 2
    rep = lambda rows: jnp.broadcast_to(jnp.concatenate(rows)[:, None], (seq - n, LANES))
    c, s = _dft_mats(n)
    leaf = (jnp.concatenate([c, s], axis=1) * seq ** -0.5).astype(BF16)
    return rep(cs_rows), rep(sn_rows), leaf


def _channel_dft():
    c, s = _dft_mats(FNET_GROUP_DIM)
    norm = FNET_GROUP_DIM ** -0.5
    eye = jnp.eye(FNET_GROUPS, dtype=F32)
    return jnp.concatenate([jnp.kron(eye, c * norm), jnp.kron(eye, -s * norm)], axis=-1).astype(BF16)


def _rope_tabs(seq):
    inv = 1.0 / (ROPE_BASE ** (jnp.arange(0, QK_ROPE_DIM, 2, dtype=F32) / QK_ROPE_DIM))
    ang = jnp.arange(seq, dtype=F32)[:, None] * inv[None, :]
    cos2 = jnp.concatenate([jnp.cos(ang), jnp.cos(ang)], axis=-1)
    sin2 = jnp.concatenate([jnp.sin(ang), jnp.sin(ang)], axis=-1)
    return _pad_rope_lanes(cos2), _pad_rope_lanes(sin2)


def _trunk(x, mem, w, tiles):
    S = x.shape[1]
    ctab, stab = _rope_tabs(S)
    twc, tws, cs = _seq_dft_tables(S)
    q_t, k, v_t, vr, vi = _proj(x, ctab, stab, w["ln_mix_g"], w["w_in_ext"], w["q_norm_g"], w["w_q_t"],
                                w["kv_norm_g"], w["w_ukv"], w["w_uv_t"], w["w_cdft"], min(tiles[0], S))
    o = _attn(q_t, k, v_t, min(tiles[1], S))
    g = _dft(vr, vi, twc, tws, cs, w["w_f"])
    return _post(x, o, g, mem, w["w_out"], w["ln_x_g"], w["mem_norm_g"], w["w_xq"], w["w_xkv"], w["w_xo"],
                 w["ln_ffn_g"], w["w_gate_up"], w["w_down"], w["final_norm_g"], min(tiles[2], S))


def kernel(x_prompt, x_sample, mem_prompt, mem_sample, ln_mix_g, w_in, q_norm_g, w_uq, kv_norm_g, w_ukv, w_fnet,
           w_out, ln_x_g, mem_norm_g, w_xq, w_xkv, w_xo, ln_ffn_g, w_gate_up, w_down, final_norm_g):
    assert ln_mix_g.shape[0] == 1, "single-layer trunk"
    w_in_ext, w_q_t, w_uv_t, w_f = _prep_weights(w_in[0], w_uq[0], w_ukv[0], w_fnet[0])
    row = lambda g: g.reshape(1, -1).astype(F32)
    w = dict(
        ln_mix_g=row(ln_mix_g[0]), w_in_ext=w_in_ext, q_norm_g=row(q_norm_g[0]), w_q_t=w_q_t,
        kv_norm_g=row(kv_norm_g[0]), w_ukv=w_ukv[0].astype(BF16), w_uv_t=w_uv_t, w_cdft=_channel_dft(), w_f=w_f,
        w_out=w_out[0].astype(BF16), ln_x_g=row(ln_x_g[0]), mem_norm_g=row(mem_norm_g[0]),
        w_xq=w_xq[0].astype(BF16), w_xkv=w_xkv[0].astype(BF16), w_xo=w_xo[0].astype(BF16),
        ln_ffn_g=row(ln_ffn_g[0]), w_gate_up=w_gate_up[0].astype(BF16), w_down=w_down[0].astype(BF16),
        final_norm_g=row(final_norm_g),
    )
    return (_trunk(x_prompt, mem_prompt, w, TILES), _trunk(x_sample, mem_sample, w, TILES))
```

```python
import functools

import jax
import jax.numpy as jnp
from jax import lax
from jax.experimental import pallas as pl
from jax.experimental.pallas import tpu as pltpu

F32 = jnp.float32
BF16 = jnp.bfloat16

D_MODEL = 1024
MLA_HEADS = 8
QK_NOPE_DIM = 64
QK_ROPE_DIM = 32
V_HEAD_DIM = 64
Q_LORA_RANK = 384
KV_LORA_RANK = 256
FNET_GROUPS = 4
FNET_GROUP_DIM = 128
FNET_WIDTH = FNET_GROUPS * FNET_GROUP_DIM
MLA_WIDTH = MLA_HEADS * V_HEAD_DIM
XATTN_HEADS = 4
XATTN_HEAD_DIM = D_MODEL // XATTN_HEADS
D_FF = 2816
ROPE_BASE = 10000.0
NORM_EPS = 1e-6
LOG2_E = 1.4426950408889634

LANES = 128
HEAD_PAD = LANES
FF_CHUNK = 256
VMEM_LIMIT = 56 * 1024 * 1024
TILES = (512, 512, 512)
ATTN_HEADS_PER_STEP = 8
DFT_LEVELS = 3
DFT_ROW_CHUNK = 256

_O_CQ = 0
_O_CKV = _O_CQ + Q_LORA_RANK
_O_KR = _O_CKV + KV_LORA_RANK
_O_U = _O_KR + HEAD_PAD
IN_EXT = _O_U + FNET_WIDTH


def _rms(x, g):
    return x * lax.rsqrt(jnp.mean(x * x, axis=-1, keepdims=True) + NORM_EPS) * g


def _dot(a, b):
    return jnp.dot(a, b, preferred_element_type=F32)


def _dot_nt(a, b):
    return lax.dot_general(a, b, (((1,), (1,)), ((), ())), preferred_element_type=F32)


def _const_spec(shape):
    zeros = (0,) * len(shape)
    return pl.BlockSpec(shape, lambda *_: zeros, pipeline_mode=pl.Buffered(1))


def _params(n_axes):
    return pltpu.CompilerParams(dimension_semantics=("arbitrary",) * n_axes,
                                vmem_limit_bytes=VMEM_LIMIT)


def _proj_kernel(x_ref, ctab_ref, stab_ref, cos_t_ref, sin_t_ref, ln_g_ref, w_in_ref, qn_g_ref, w_q_t_ref,
                 kvn_g_ref, w_ukv_ref, w_uv_t_ref, w_cdft_ref, q_t_ref, k_ref, v_t_ref, vr_ref, vi_ref):
    scale = (QK_NOPE_DIM + QK_ROPE_DIM) ** -0.5 * LOG2_E
    half = QK_ROPE_DIM // 2
    qk_dim = QK_NOPE_DIM + QK_ROPE_DIM
    h = _rms(x_ref[0], ln_g_ref[...]).astype(BF16)
    z = _dot(h, w_in_ref[...])

    cq = _rms(z[:, _O_CQ:_O_CQ + Q_LORA_RANK], qn_g_ref[...]).astype(BF16)
    q_t = _dot_nt(w_q_t_ref[...], cq)
    cos_t = cos_t_ref[...] * scale
    sin_t = sin_t_ref[...] * scale
    zero_rows = jnp.zeros((HEAD_PAD - qk_dim, q_t.shape[1]), BF16)
    for hd in range(MLA_HEADS):
        src, dst = hd * qk_dim, hd * HEAD_PAD
        x1 = q_t[src + QK_NOPE_DIM:src + QK_NOPE_DIM + half, :]
        x2 = q_t[src + QK_NOPE_DIM + half:src + qk_dim, :]
        q_t_ref[0, dst:dst + QK_NOPE_DIM, :] = (q_t[src:src + QK_NOPE_DIM, :] * scale).astype(BF16)
        q_t_ref[0, dst + QK_NOPE_DIM:dst + QK_NOPE_DIM + half, :] = (x1 * cos_t - x2 * sin_t).astype(BF16)
        q_t_ref[0, dst + QK_NOPE_DIM + half:dst + qk_dim, :] = (x2 * cos_t + x1 * sin_t).astype(BF16)
        q_t_ref[0, dst + qk_dim:dst + HEAD_PAD, :] = zero_rows

    ckv = _rms(z[:, _O_CKV:_O_CKV + KV_LORA_RANK], kvn_g_ref[...]).astype(BF16)
    kv = _dot(ckv, w_ukv_ref[...])
    zk = z[:, _O_KR:_O_KR + HEAD_PAD]
    kr = zk * ctab_ref[...] + pltpu.roll(zk, HEAD_PAD - QK_ROPE_DIM, 1) * stab_ref[...]
    nope = lax.broadcasted_iota(jnp.int32, kr.shape, 1) < QK_NOPE_DIM
    for hd in range(MLA_HEADS):
        sl = slice(hd * HEAD_PAD, (hd + 1) * HEAD_PAD)
        k_ref[0, :, sl] = jnp.where(nope, kv[:, sl], kr).astype(BF16)
    v_t_ref[0] = _dot_nt(w_uv_t_ref[...], ckv).astype(BF16)

    for gi in range(FNET_GROUPS):
        sl = slice(gi * FNET_GROUP_DIM, (gi + 1) * FNET_GROUP_DIM)
        v = _dot(z[:, _O_U + gi * FNET_GROUP_DIM:_O_U + (gi + 1) * FNET_GROUP_DIM].astype(BF16), w_cdft_ref[...])
        vr_ref[0, :, sl] = v[:, :FNET_GROUP_DIM].astype(BF16)
        vi_ref[0, :, sl] = v[:, FNET_GROUP_DIM:].astype(BF16)


def _proj(x, ctab, stab, cos_t, sin_t, ln_g, w_in_ext, qn_g, w_q_t, kvn_g, w_ukv, w_uv_t, w_cdft, tile):
    B, S, _ = x.shape
    hp = MLA_HEADS * HEAD_PAD
    tok = lambda w: pl.BlockSpec((1, tile, w), lambda b, t: (b, t, 0))
    tok_t = lambda w: pl.BlockSpec((1, w, tile), lambda b, t: (b, 0, t))
    tab = pl.BlockSpec((tile, LANES), lambda b, t: (t, 0))
    tab_t = pl.BlockSpec((cos_t.shape[0], tile), lambda b, t: (0, t))
    return pl.pallas_call(
        _proj_kernel,
        grid=(B, S // tile),
        in_specs=[tok(D_MODEL), tab, tab, tab_t, tab_t, _const_spec(ln_g.shape), _const_spec(w_in_ext.shape),
                  _const_spec(qn_g.shape), _const_spec(w_q_t.shape), _const_spec(kvn_g.shape),
                  _const_spec(w_ukv.shape), _const_spec(w_uv_t.shape), _const_spec(w_cdft.shape)],
        out_specs=[tok_t(hp), tok(hp), tok_t(MLA_WIDTH), tok(FNET_WIDTH), tok(FNET_WIDTH)],
        out_shape=[jax.ShapeDtypeStruct((B, hp, S), BF16), jax.ShapeDtypeStruct((B, S, hp), BF16),
                   jax.ShapeDtypeStruct((B, MLA_WIDTH, S), BF16)]
                  + [jax.ShapeDtypeStruct((B, S, FNET_WIDTH), BF16)] * 2,
        compiler_params=_params(2),
        name="proj",
    )(x, ctab, stab, cos_t, sin_t, ln_g, w_in_ext, qn_g, w_q_t, kvn_g, w_ukv, w_uv_t, w_cdft)


def _attn_kernel(q_t_ref, k_ref, v_t_ref, o_ref):
    def scores(hh):
        return _dot(k_ref[0, :, hh * HEAD_PAD:(hh + 1) * HEAD_PAD],
                    q_t_ref[0, hh * HEAD_PAD:(hh + 1) * HEAD_PAD, :])

    outs = []
    s_next = scores(0)
    for hh in range(ATTN_HEADS_PER_STEP):
        s = s_next
        if hh + 1 < ATTN_HEADS_PER_STEP:
            s_next = scores(hh + 1)
        m = jnp.max(s, axis=0, keepdims=True)
        p = jnp.exp2(s - m)
        l = jnp.sum(p, axis=0, keepdims=True)
        o_t = _dot(v_t_ref[0, hh * V_HEAD_DIM:(hh + 1) * V_HEAD_DIM, :], p.astype(BF16))
        outs.append(o_t / l)
    o_ref[0] = jnp.concatenate(outs, axis=0).T.astype(BF16)


def _attn(q_t, k, v_t, tile):
    B, S, _ = k.shape
    hps = ATTN_HEADS_PER_STEP
    return pl.pallas_call(
        _attn_kernel,
        grid=(B, MLA_HEADS // hps, S // tile),
        in_specs=[pl.BlockSpec((1, hps * HEAD_PAD, tile), lambda b, j, t: (b, j, t)),
                  pl.BlockSpec((1, S, hps * HEAD_PAD), lambda b, j, t: (b, 0, j)),
                  pl.BlockSpec((1, hps * V_HEAD_DIM, S), lambda b, j, t: (b, j, 0))],
        out_specs=pl.BlockSpec((1, tile, hps * V_HEAD_DIM), lambda b, j, t: (b, t, j)),
        out_shape=jax.ShapeDtypeStruct((B, S, MLA_WIDTH), BF16),
        compiler_params=_params(3),
        name="attn",
    )(q_t, k, v_t)


def _bit_reverse(i, bits):
    return int(format(i, "0%db" % bits)[::-1], 2)


def _dft_kernel(vr_ref, vi_ref, twc_ref, tws_ref, cs_ref, w_f_ref, g_ref, xr_scr, xi_scr, xb_scr, g_scr):
    S = vr_ref.shape[1]
    n_leaves = 2 ** DFT_LEVELS
    leaf = S // n_leaves
    chunk = min(DFT_ROW_CHUNK, leaf)
    reps = FNET_WIDTH // LANES
    n, off = S, 0
    for lev in range(DFT_LEVELS):
        half = n // 2
        last = lev == DFT_LEVELS - 1
        for base in range(0, S, n):
            for c0 in range(0, half, chunk):
                top = slice(base + c0, base + c0 + chunk)
                bot = slice(base + half + c0, base + half + c0 + chunk)
                if lev == 0:
                    tr, ti = vr_ref[0, top, :].astype(F32), vi_ref[0, top, :].astype(F32)
                    br, bi = vr_ref[0, bot, :].astype(F32), vi_ref[0, bot, :].astype(F32)
                else:
                    tr, ti, br, bi = xr_scr[top, :], xi_scr[top, :], xr_scr[bot, :], xi_scr[bot, :]
                c = jnp.concatenate([twc_ref[off + c0:off + c0 + chunk, :]] * reps, axis=1)
                s = jnp.concatenate([tws_ref[off + c0:off + c0 + chunk, :]] * reps, axis=1)
                dr, di = tr - br, ti - bi
                ar, ai = tr + br, ti + bi
                mr, mi = dr * c + di * s, di * c - dr * s
                if last:
                    blk, r0 = (base // n) * 2, c0
                    xb_scr[blk, r0:r0 + chunk, :] = ar.astype(BF16)
                    xb_scr[blk, leaf + r0:leaf + r0 + chunk, :] = ai.astype(BF16)
                    xb_scr[blk + 1, r0:r0 + chunk, :] = mr.astype(BF16)
                    xb_scr[blk + 1, leaf + r0:leaf + r0 + chunk, :] = mi.astype(BF16)
                else:
                    xr_scr[top, :], xi_scr[top, :] = ar, ai
                    xr_scr[bot, :], xi_scr[bot, :] = mr, mi
        off += half
        n = half
    for blk in range(n_leaves):
        f = _dot(cs_ref[...], xb_scr[blk])
        g = _dot(f.astype(BF16), w_f_ref[...])
        r = _bit_reverse(blk, DFT_LEVELS)
        for lt in range(reps):
            g_scr[lt, pl.ds(r, leaf, stride=n_leaves), :] = g[:, lt * LANES:(lt + 1) * LANES]
    for lt in range(reps):
        g_ref[0, :, lt * LANES:(lt + 1) * LANES] = g_scr[lt].astype(BF16)


def _dft(vr, vi, twc, tws, cs, w_f):
    B, S, _ = vr.shape
    n_leaves = 2 ** DFT_LEVELS
    leaf = S // n_leaves
    bat = pl.BlockSpec((1, S, FNET_WIDTH), lambda b: (b, 0, 0))
    return pl.pallas_call(
        _dft_kernel,
        grid=(B,),
        in_specs=[bat, bat, _const_spec(twc.shape), _const_spec(tws.shape), _const_spec(cs.shape),
                  _const_spec(w_f.shape)],
        out_specs=bat,
        out_shape=jax.ShapeDtypeStruct((B, S, FNET_WIDTH), BF16),
        scratch_shapes=[pltpu.VMEM((S, FNET_WIDTH), F32), pltpu.VMEM((S, FNET_WIDTH), F32),
                        pltpu.VMEM((n_leaves, 2 * leaf, FNET_WIDTH), BF16),
                        pltpu.VMEM((FNET_WIDTH // LANES, S, LANES), F32)],
        compiler_params=_params(1),
        name="dft",
    )(vr, vi, twc, tws, cs, w_f)


def _post_kernel(x_ref, o_ref, g_ref, mem_ref, w_out_ref, ln_x_g_ref, mem_g_ref, w_xq_ref, w_xkv_ref,
                 w_xo_ref, ln_f_g_ref, w_gu_ref, w_d_ref, fin_g_ref, y_ref, kv_scr, a_scr):
    @pl.when(pl.program_id(1) == 0)
    def _():
        m = _rms(mem_ref[0], mem_g_ref[...]).astype(BF16)
        kv_scr[...] = _dot(m, w_xkv_ref[...]).astype(BF16)

    x1 = (x_ref[0] + _dot(o_ref[0], w_out_ref[:MLA_WIDTH, :])
          + _dot(g_ref[0], w_out_ref[MLA_WIDTH:, :]))

    hq = _rms(x1, ln_x_g_ref[...]).astype(BF16)
    q = (_dot(hq, w_xq_ref[...]) * (XATTN_HEAD_DIM ** -0.5)).astype(BF16)
    heads = []
    for hd in range(XATTN_HEADS):
        sl = slice(hd * XATTN_HEAD_DIM, (hd + 1) * XATTN_HEAD_DIM)
        s = _dot_nt(q[:, sl], kv_scr[:, sl])
        m = jnp.max(s, axis=-1, keepdims=True)
        p = jnp.exp(s - m)
        l = jnp.sum(p, axis=-1, keepdims=True)
        vh = kv_scr[:, D_MODEL + hd * XATTN_HEAD_DIM:D_MODEL + (hd + 1) * XATTN_HEAD_DIM]
        heads.append((_dot(p.astype(BF16), vh) / l).astype(BF16))
    x2 = x1 + _dot(jnp.concatenate(heads, axis=-1), w_xo_ref[...])

    hf = _rms(x2, ln_f_g_ref[...]).astype(BF16)
    for c in range(D_FF // FF_CHUNK):
        sl = slice(c * FF_CHUNK, (c + 1) * FF_CHUNK)
        gate = _dot(hf, w_gu_ref[:, sl])
        up = _dot(hf, w_gu_ref[:, D_FF + c * FF_CHUNK:D_FF + (c + 1) * FF_CHUNK])
        a_scr[:, sl] = (gate * jax.nn.sigmoid(gate) * up).astype(BF16)
    x3 = x2 + _dot(a_scr[...], w_d_ref[...])
    y_ref[0] = _rms(x3, fin_g_ref[...])


def _post(x, o, g, mem, w_out, ln_x_g, mem_g, w_xq, w_xkv, w_xo, ln_f_g, w_gu, w_d, fin_g, tile):
    B, S, _ = x.shape
    M = mem.shape[1]
    tok = lambda w: pl.BlockSpec((1, tile, w), lambda b, t: (b, t, 0))
    consts = [w_out, ln_x_g, mem_g, w_xq, w_xkv, w_xo, ln_f_g, w_gu, w_d, fin_g]
    return pl.pallas_call(
        _post_kernel,
        grid=(B, S // tile),
        in_specs=[tok(D_MODEL), tok(MLA_WIDTH), tok(FNET_WIDTH),
                  pl.BlockSpec((1, M, D_MODEL), lambda b, t: (b, 0, 0))]
                 + [_const_spec(c.shape) for c in consts],
        out_specs=tok(D_MODEL),
        out_shape=jax.ShapeDtypeStruct((B, S, D_MODEL), F32),
        scratch_shapes=[pltpu.VMEM((M, 2 * D_MODEL), BF16), pltpu.VMEM((tile, D_FF), BF16)],
        compiler_params=_params(2),
        name="post",
    )(x, o, g, mem, *consts)


def _rot_cols(w):
    half = QK_ROPE_DIM // 2
    return jnp.concatenate([-w[..., half:], w[..., :half]], axis=-1)


def _prep_weights(w_in, w_uq, w_ukv, w_fnet):
    w_cq = w_in[:, :Q_LORA_RANK]
    w_ckv = w_in[:, Q_LORA_RANK:Q_LORA_RANK + KV_LORA_RANK]
    o3 = Q_LORA_RANK + KV_LORA_RANK
    w_kr = w_in[:, o3:o3 + QK_ROPE_DIM]
    w_u = w_in[:, o3 + QK_ROPE_DIM:]
    w_kr_grp = jnp.concatenate([jnp.zeros((D_MODEL, QK_NOPE_DIM), F32), w_kr, _rot_cols(w_kr)], axis=-1)
    w_in_ext = jnp.concatenate([w_cq, w_ckv, w_kr_grp, w_u], axis=-1).astype(BF16)

    w_q_t = w_uq.T.astype(BF16)

    kv_w = w_ukv.reshape(KV_LORA_RANK, MLA_HEADS, QK_NOPE_DIM + V_HEAD_DIM)
    w_uv_t = kv_w[..., QK_NOPE_DIM:].reshape(KV_LORA_RANK, MLA_WIDTH).T.astype(BF16)

    w_f = jnp.zeros((FNET_WIDTH, FNET_WIDTH), F32)
    for gi in range(FNET_GROUPS):
        sl = slice(gi * FNET_GROUP_DIM, (gi + 1) * FNET_GROUP_DIM)
        w_f = w_f.at[sl, sl].set(w_fnet[gi])
    return w_in_ext, w_q_t, w_uv_t, w_f.astype(BF16)


def _dft_mats(n):
    idx = jnp.arange(n, dtype=jnp.int32)
    ang = ((idx[:, None] * idx[None, :]) % n).astype(F32) * (2.0 * jnp.pi / n)
    return jnp.cos(ang), jnp.sin(ang)


def _seq_dft_tables(seq):
    cs_rows, sn_rows = [], []
    n = seq
    for _ in range(DFT_LEVELS):
        ang = jnp.arange(n // 2, dtype=F32) * (2.0 * jnp.pi / n)
        cs_rows.append(jnp.cos(ang))
        sn_rows.append(jnp.sin(ang))
        n //= 2
    rep = lambda rows: jnp.broadcast_to(jnp.concatenate(rows)[:, None], (seq - n, LANES))
    c, s = _dft_mats(n)
    leaf = (jnp.concatenate([c, s], axis=1) * seq ** -0.5).astype(BF16)
    return rep(cs_rows), rep(sn_rows), leaf


def _channel_dft():
    c, s = _dft_mats(FNET_GROUP_DIM)
    return (jnp.concatenate([c, -s], axis=-1) * FNET_GROUP_DIM ** -0.5).astype(BF16)


def _rope_tabs(seq):
    inv = 1.0 / (ROPE_BASE ** (jnp.arange(0, QK_ROPE_DIM, 2, dtype=F32) / QK_ROPE_DIM))
    ang = jnp.arange(seq, dtype=F32)[:, None] * inv[None, :]
    cos, sin = jnp.cos(ang), jnp.sin(ang)
    pad = lambda t: jnp.concatenate([jnp.zeros((seq, QK_NOPE_DIM), F32), t, t,
                                     jnp.zeros((seq, HEAD_PAD - QK_NOPE_DIM - QK_ROPE_DIM), F32)], axis=-1)
    return pad(cos), pad(sin), cos.T, sin.T


def _trunk(x, mem, w, tiles):
    S = x.shape[1]
    ctab, stab, cos_t, sin_t = _rope_tabs(S)
    twc, tws, cs = _seq_dft_tables(S)
    q_t, k, v_t, vr, vi = _proj(x, ctab, stab, cos_t, sin_t, w["ln_mix_g"], w["w_in_ext"], w["q_norm_g"],
                                w["w_q_t"], w["kv_norm_g"], w["w_ukv"], w["w_uv_t"], w["w_cdft"],
                                min(tiles[0], S))
    o = _attn(q_t, k, v_t, min(tiles[1], S))
    g = _dft(vr, vi, twc, tws, cs, w["w_f"])
    return _post(x, o, g, mem, w["w_out"], w["ln_x_g"], w["mem_norm_g"], w["w_xq"], w["w_xkv"], w["w_xo"],
                 w["ln_ffn_g"], w["w_gate_up"], w["w_down"], w["final_norm_g"], min(tiles[2], S))


def kernel(x_prompt, x_sample, mem_prompt, mem_sample, ln_mix_g, w_in, q_norm_g, w_uq, kv_norm_g, w_ukv, w_fnet,
           w_out, ln_x_g, mem_norm_g, w_xq, w_xkv, w_xo, ln_ffn_g, w_gate_up, w_down, final_norm_g):
    assert ln_mix_g.shape[0] == 1, "single-layer trunk"
    w_in_ext, w_q_t, w_uv_t, w_f = _prep_weights(w_in[0], w_uq[0], w_ukv[0], w_fnet[0])
    row = lambda g: g.reshape(1, -1).astype(F32)
    w = dict(
        ln_mix_g=row(ln_mix_g[0]), w_in_ext=w_in_ext, q_norm_g=row(q_norm_g[0]), w_q_t=w_q_t,
        kv_norm_g=row(kv_norm_g[0]), w_ukv=w_ukv[0].astype(BF16), w_uv_t=w_uv_t, w_cdft=_channel_dft(), w_f=w_f,
        w_out=w_out[0].astype(BF16), ln_x_g=row(ln_x_g[0]), mem_norm_g=row(mem_norm_g[0]),
        w_xq=w_xq[0].astype(BF16), w_xkv=w_xkv[0].astype(BF16), w_xo=w_xo[0].astype(BF16),
        ln_ffn_g=row(ln_ffn_g[0]), w_gate_up=w_gate_up[0].astype(BF16), w_down=w_down[0].astype(BF16),
        final_norm_g=row(final_norm_g),
    )
    return (_trunk(x_prompt, mem_prompt, w, TILES), _trunk(x_sample, mem_sample, w, TILES))
```

```python
import functools

import jax
import jax.numpy as jnp
from jax import lax
from jax.experimental import pallas as pl
from jax.experimental.pallas import tpu as pltpu

F32 = jnp.float32
BF16 = jnp.bfloat16

D_MODEL = 1024
MLA_HEADS = 8
QK_NOPE_DIM = 64
QK_ROPE_DIM = 32
V_HEAD_DIM = 64
Q_LORA_RANK = 384
KV_LORA_RANK = 256
FNET_GROUPS = 4
FNET_GROUP_DIM = 128
FNET_WIDTH = FNET_GROUPS * FNET_GROUP_DIM
MLA_WIDTH = MLA_HEADS * V_HEAD_DIM
XATTN_HEADS = 4
XATTN_HEAD_DIM = D_MODEL // XATTN_HEADS
D_FF = 2816
ROPE_BASE = 10000.0
NORM_EPS = 1e-6
LOG2_E = 1.4426950408889634

LANES = 128
HEAD_PAD = LANES
V_ROWS = V_HEAD_DIM
FF_CHUNK = 256
VMEM_LIMIT = 56 * 1024 * 1024
TILES = (512, 512, 512)
ATTN_HEADS_PER_STEP = 8
ATTN_KEY_CHUNK = 1024
ATTN_LAG_CHUNKS = 0
ATTN_SCORE_SLOTS = 2
DFT_LEVELS = 3
DFT_ROW_CHUNK = 256

_O_CQ = 0
_O_CKV = _O_CQ + Q_LORA_RANK
_O_KR = _O_CKV + KV_LORA_RANK
_O_U = _O_KR + HEAD_PAD
IN_EXT = _O_U + FNET_WIDTH


def _rms(x, g):
    return x * lax.rsqrt(jnp.mean(x * x, axis=-1, keepdims=True) + NORM_EPS) * g


def _dot(a, b):
    return jnp.dot(a, b, preferred_element_type=F32)


def _dot_nt(a, b):
    return lax.dot_general(a, b, (((1,), (1,)), ((), ())), preferred_element_type=F32)


def _const_spec(shape):
    zeros = (0,) * len(shape)
    return pl.BlockSpec(shape, lambda *_: zeros, pipeline_mode=pl.Buffered(1))


def _params(n_axes, flags=None):
    return pltpu.CompilerParams(dimension_semantics=("arbitrary",) * n_axes,
                                vmem_limit_bytes=VMEM_LIMIT, flags=flags)


def _proj_kernel(x_ref, ctab_ref, stab_ref, cos_t_ref, sin_t_ref, ln_g_ref, w_in_ref, qn_g_ref, w_q_t_ref,
                 kvn_g_ref, w_ukv_ref, w_uv_t_ref, w_cdft_ref, q_t_ref, k_ref, v_t_ref, vr_ref, vi_ref):
    scale = (QK_NOPE_DIM + QK_ROPE_DIM) ** -0.5 * LOG2_E
    half = QK_ROPE_DIM // 2
    qk_dim = QK_NOPE_DIM + QK_ROPE_DIM
    h = _rms(x_ref[0], ln_g_ref[...]).astype(BF16)
    z = _dot(h, w_in_ref[...])

    cq = _rms(z[:, _O_CQ:_O_CQ + Q_LORA_RANK], qn_g_ref[...]).astype(BF16)
    q_t = _dot_nt(w_q_t_ref[...], cq)
    cos_t = cos_t_ref[...] * scale
    sin_t = sin_t_ref[...] * scale
    zero_rows = jnp.zeros((HEAD_PAD - qk_dim, q_t.shape[1]), BF16)
    for hd in range(MLA_HEADS):
        src, dst = hd * qk_dim, hd * HEAD_PAD
        x1 = q_t[src + QK_NOPE_DIM:src + QK_NOPE_DIM + half, :]
        x2 = q_t[src + QK_NOPE_DIM + half:src + qk_dim, :]
        q_t_ref[0, dst:dst + QK_NOPE_DIM, :] = (q_t[src:src + QK_NOPE_DIM, :] * scale).astype(BF16)
        q_t_ref[0, dst + QK_NOPE_DIM:dst + QK_NOPE_DIM + half, :] = (x1 * cos_t - x2 * sin_t).astype(BF16)
        q_t_ref[0, dst + QK_NOPE_DIM + half:dst + qk_dim, :] = (x2 * cos_t + x1 * sin_t).astype(BF16)
        q_t_ref[0, dst + qk_dim:dst + HEAD_PAD, :] = zero_rows

    ckv = _rms(z[:, _O_CKV:_O_CKV + KV_LORA_RANK], kvn_g_ref[...]).astype(BF16)
    kv = _dot(ckv, w_ukv_ref[...])
    zk = z[:, _O_KR:_O_KR + HEAD_PAD]
    kr = zk * ctab_ref[...] + pltpu.roll(zk, HEAD_PAD - QK_ROPE_DIM, 1) * stab_ref[...]
    nope = lax.broadcasted_iota(jnp.int32, kr.shape, 1) < QK_NOPE_DIM
    for hd in range(MLA_HEADS):
        sl = slice(hd * HEAD_PAD, (hd + 1) * HEAD_PAD)
        k_ref[0, :, sl] = jnp.where(nope, kv[:, sl], kr).astype(BF16)
    v_t = _dot_nt(w_uv_t_ref[...], ckv).astype(BF16)
    for hd in range(MLA_HEADS):
        v_t_ref[0, hd * V_ROWS:hd * V_ROWS + V_HEAD_DIM, :] = v_t[hd * V_HEAD_DIM:(hd + 1) * V_HEAD_DIM, :]
        if V_ROWS > V_HEAD_DIM:
            ones_grp = lax.broadcasted_iota(jnp.int32, (V_ROWS - V_HEAD_DIM, v_t.shape[1]), 0) == 0
            v_t_ref[0, hd * V_ROWS + V_HEAD_DIM:(hd + 1) * V_ROWS, :] = ones_grp.astype(BF16)

    for gi in range(FNET_GROUPS):
        sl = slice(gi * FNET_GROUP_DIM, (gi + 1) * FNET_GROUP_DIM)
        v = _dot(z[:, _O_U + gi * FNET_GROUP_DIM:_O_U + (gi + 1) * FNET_GROUP_DIM].astype(BF16), w_cdft_ref[...])
        vr_ref[0, :, sl] = v[:, :FNET_GROUP_DIM].astype(BF16)
        vi_ref[0, :, sl] = v[:, FNET_GROUP_DIM:].astype(BF16)


def _proj(x, ctab, stab, cos_t, sin_t, ln_g, w_in_ext, qn_g, w_q_t, kvn_g, w_ukv, w_uv_t, w_cdft, tile):
    B, S, _ = x.shape
    hp = MLA_HEADS * HEAD_PAD
    tok = lambda w: pl.BlockSpec((1, tile, w), lambda b, t: (b, t, 0))
    tok_t = lambda w: pl.BlockSpec((1, w, tile), lambda b, t: (b, 0, t))
    tab = pl.BlockSpec((tile, LANES), lambda b, t: (t, 0))
    tab_t = pl.BlockSpec((cos_t.shape[0], tile), lambda b, t: (0, t))
    return pl.pallas_call(
        _proj_kernel,
        grid=(B, S // tile),
        in_specs=[tok(D_MODEL), tab, tab, tab_t, tab_t, _const_spec(ln_g.shape), _const_spec(w_in_ext.shape),
                  _const_spec(qn_g.shape), _const_spec(w_q_t.shape), _const_spec(kvn_g.shape),
                  _const_spec(w_ukv.shape), _const_spec(w_uv_t.shape), _const_spec(w_cdft.shape)],
        out_specs=[tok_t(hp), tok(hp), tok_t(MLA_HEADS * V_ROWS), tok(FNET_WIDTH), tok(FNET_WIDTH)],
        out_shape=[jax.ShapeDtypeStruct((B, hp, S), BF16), jax.ShapeDtypeStruct((B, S, hp), BF16),
                   jax.ShapeDtypeStruct((B, MLA_HEADS * V_ROWS, S), BF16)]
                  + [jax.ShapeDtypeStruct((B, S, FNET_WIDTH), BF16)] * 2,
        compiler_params=_params(2),
        name="proj",
    )(x, ctab, stab, cos_t, sin_t, ln_g, w_in_ext, qn_g, w_q_t, kvn_g, w_ukv, w_uv_t, w_cdft)


def _attn_kernel(q_t_ref, k_ref, v_t_ref, o_ref, s_scr):
    S = k_ref.shape[1]
    kc = min(ATTN_KEY_CHUNK, S)
    n_chunks = S // kc

    def score_chunk(hh, c, slot):
        s = _dot(k_ref[0, c * kc:(c + 1) * kc, hh * HEAD_PAD:(hh + 1) * HEAD_PAD],
                 q_t_ref[0, hh * HEAD_PAD:(hh + 1) * HEAD_PAD, :])
        s_scr[slot, c * kc:(c + 1) * kc, :] = s
        return jnp.max(s, axis=0, keepdims=True)

    def pv_chunk(hh, c, slot, m):
        p = jnp.exp2(s_scr[slot, c * kc:(c + 1) * kc, :] - m)
        o = _dot(v_t_ref[0, hh * V_ROWS:(hh + 1) * V_ROWS, c * kc:(c + 1) * kc], p.astype(BF16))
        return o, jnp.sum(p, axis=0, keepdims=True)

    items = [(hh, c) for hh in range(ATTN_HEADS_PER_STEP) for c in range(n_chunks)]
    lag = n_chunks + ATTN_LAG_CHUNKS
    m_parts = [[] for _ in range(ATTN_HEADS_PER_STEP)]
    m_head = [None] * ATTN_HEADS_PER_STEP
    acc = [None] * ATTN_HEADS_PER_STEP
    den = [None] * ATTN_HEADS_PER_STEP
    for i in range(len(items) + lag):
        if i < len(items):
            hh, c = items[i]
            m_parts[hh].append(score_chunk(hh, c, hh % ATTN_SCORE_SLOTS))
            if c == n_chunks - 1:
                m_head[hh] = functools.reduce(jnp.maximum, m_parts[hh])
        if i >= lag:
            hh, c = items[i - lag]
            o_c, l_c = pv_chunk(hh, c, hh % ATTN_SCORE_SLOTS, m_head[hh])
            acc[hh] = o_c if acc[hh] is None else acc[hh] + o_c
            den[hh] = l_c if den[hh] is None else den[hh] + l_c
    outs = [a / l for a, l in zip(acc, den)]
    o_ref[0] = jnp.concatenate(outs, axis=0).T.astype(BF16)


def _attn(q_t, k, v_t, tile):
    B, S, _ = k.shape
    hps = ATTN_HEADS_PER_STEP
    return pl.pallas_call(
        _attn_kernel,
        grid=(B, MLA_HEADS // hps, S // tile),
        in_specs=[pl.BlockSpec((1, hps * HEAD_PAD, tile), lambda b, j, t: (b, j, t)),
                  pl.BlockSpec((1, S, hps * HEAD_PAD), lambda b, j, t: (b, 0, j)),
                  pl.BlockSpec((1, hps * V_ROWS, S), lambda b, j, t: (b, j, 0))],
        out_specs=pl.BlockSpec((1, tile, hps * V_HEAD_DIM), lambda b, j, t: (b, t, j)),
        out_shape=jax.ShapeDtypeStruct((B, S, MLA_WIDTH), BF16),
        scratch_shapes=[pltpu.VMEM((ATTN_SCORE_SLOTS, S, tile), F32)],
        compiler_params=_params(3),
        name="attn",
    )(q_t, k, v_t)


def _bit_reverse(i, bits):
    return int(format(i, "0%db" % bits)[::-1], 2)


def _dft_kernel(vr_ref, vi_ref, twc_ref, tws_ref, cs_ref, w_f_ref, g_ref, xr_scr, xi_scr, xb_scr, g_scr):
    S = vr_ref.shape[1]
    n_leaves = 2 ** DFT_LEVELS
    leaf = S // n_leaves
    chunk = min(DFT_ROW_CHUNK, leaf)
    reps = FNET_WIDTH // LANES
    n, off = S, 0
    for lev in range(DFT_LEVELS):
        half = n // 2
        last = lev == DFT_LEVELS - 1
        for base in range(0, S, n):
            for c0 in range(0, half, chunk):
                top = slice(base + c0, base + c0 + chunk)
                bot = slice(base + half + c0, base + half + c0 + chunk)
                if lev == 0:
                    tr, ti = vr_ref[0, top, :].astype(F32), vi_ref[0, top, :].astype(F32)
                    br, bi = vr_ref[0, bot, :].astype(F32), vi_ref[0, bot, :].astype(F32)
                else:
                    tr, ti, br, bi = xr_scr[top, :], xi_scr[top, :], xr_scr[bot, :], xi_scr[bot, :]
                c = jnp.concatenate([twc_ref[off + c0:off + c0 + chunk, :]] * reps, axis=1)
                s = jnp.concatenate([tws_ref[off + c0:off + c0 + chunk, :]] * reps, axis=1)
                dr, di = tr - br, ti - bi
                ar, ai = tr + br, ti + bi
                mr, mi = dr * c + di * s, di * c - dr * s
                if last:
                    blk, r0 = (base // n) * 2, c0
                    xb_scr[blk, r0:r0 + chunk, :] = ar.astype(BF16)
                    xb_scr[blk, leaf + r0:leaf + r0 + chunk, :] = ai.astype(BF16)
                    xb_scr[blk + 1, r0:r0 + chunk, :] = mr.astype(BF16)
                    xb_scr[blk + 1, leaf + r0:leaf + r0 + chunk, :] = mi.astype(BF16)
                else:
                    xr_scr[top, :], xi_scr[top, :] = ar, ai
                    xr_scr[bot, :], xi_scr[bot, :] = mr, mi
        off += half
        n = half
    for blk in range(n_leaves):
        f = _dot(cs_ref[...], xb_scr[blk])
        g = _dot(f.astype(BF16), w_f_ref[...])
        r = _bit_reverse(blk, DFT_LEVELS)
        for lt in range(reps):
            g_scr[lt, pl.ds(r, leaf, stride=n_leaves), :] = g[:, lt * LANES:(lt + 1) * LANES]
    for lt in range(reps):
        g_ref[0, :, lt * LANES:(lt + 1) * LANES] = g_scr[lt].astype(BF16)


def _dft(vr, vi, twc, tws, cs, w_f):
    B, S, _ = vr.shape
    n_leaves = 2 ** DFT_LEVELS
    leaf = S // n_leaves
    bat = pl.BlockSpec((1, S, FNET_WIDTH), lambda b: (b, 0, 0))
    return pl.pallas_call(
        _dft_kernel,
        grid=(B,),
        in_specs=[bat, bat, _const_spec(twc.shape), _const_spec(tws.shape), _const_spec(cs.shape),
                  _const_spec(w_f.shape)],
        out_specs=bat,
        out_shape=jax.ShapeDtypeStruct((B, S, FNET_WIDTH), BF16),
        scratch_shapes=[pltpu.VMEM((S, FNET_WIDTH), F32), pltpu.VMEM((S, FNET_WIDTH), F32),
                        pltpu.VMEM((n_leaves, 2 * leaf, FNET_WIDTH), BF16),
                        pltpu.VMEM((FNET_WIDTH // LANES, S, LANES), F32)],
        compiler_params=_params(1),
        name="dft",
    )(vr, vi, twc, tws, cs, w_f)


def _post_kernel(x_ref, o_ref, g_ref, mem_ref, w_out_ref, ln_x_g_ref, mem_g_ref, w_xq_ref, w_xkv_ref,
                 w_xo_ref, ln_f_g_ref, w_gu_ref, w_d_ref, fin_g_ref, y_ref, kv_scr, a_scr):
    @pl.when(pl.program_id(1) == 0)
    def _():
        m = _rms(mem_ref[0], mem_g_ref[...]).astype(BF16)
        kv_scr[...] = _dot(m, w_xkv_ref[...]).astype(BF16)

    x1 = (x_ref[0] + _dot(o_ref[0], w_out_ref[:MLA_WIDTH, :])
          + _dot(g_ref[0], w_out_ref[MLA_WIDTH:, :]))

    hq = _rms(x1, ln_x_g_ref[...]).astype(BF16)
    q = (_dot(hq, w_xq_ref[...]) * (XATTN_HEAD_DIM ** -0.5)).astype(BF16)
    heads = []
    for hd in range(XATTN_HEADS):
        sl = slice(hd * XATTN_HEAD_DIM, (hd + 1) * XATTN_HEAD_DIM)
        s = _dot_nt(q[:, sl], kv_scr[:, sl])
        m = jnp.max(s, axis=-1, keepdims=True)
        p = jnp.exp(s - m)
        l = jnp.sum(p, axis=-1, keepdims=True)
        vh = kv_scr[:, D_MODEL + hd * XATTN_HEAD_DIM:D_MODEL + (hd + 1) * XATTN_HEAD_DIM]
        heads.append((_dot(p.astype(BF16), vh) / l).astype(BF16))
    x2 = x1 + _dot(jnp.concatenate(heads, axis=-1), w_xo_ref[...])

    hf = _rms(x2, ln_f_g_ref[...]).astype(BF16)
    for c in range(D_FF // FF_CHUNK):
        sl = slice(c * FF_CHUNK, (c + 1) * FF_CHUNK)
        gate = _dot(hf, w_gu_ref[:, sl])
        up = _dot(hf, w_gu_ref[:, D_FF + c * FF_CHUNK:D_FF + (c + 1) * FF_CHUNK])
        a_scr[:, sl] = (gate * jax.nn.sigmoid(gate) * up).astype(BF16)
    x3 = x2 + _dot(a_scr[...], w_d_ref[...])
    y_ref[0] = _rms(x3, fin_g_ref[...])


def _post(x, o, g, mem, w_out, ln_x_g, mem_g, w_xq, w_xkv, w_xo, ln_f_g, w_gu, w_d, fin_g, tile):
    B, S, _ = x.shape
    M = mem.shape[1]
    tok = lambda w: pl.BlockSpec((1, tile, w), lambda b, t: (b, t, 0))
    consts = [w_out, ln_x_g, mem_g, w_xq, w_xkv, w_xo, ln_f_g, w_gu, w_d, fin_g]
    return pl.pallas_call(
        _post_kernel,
        grid=(B, S // tile),
        in_specs=[tok(D_MODEL), tok(MLA_WIDTH), tok(FNET_WIDTH),
                  pl.BlockSpec((1, M, D_MODEL), lambda b, t: (b, 0, 0))]
                 + [_const_spec(c.shape) for c in consts],
        out_specs=tok(D_MODEL),
        out_shape=jax.ShapeDtypeStruct((B, S, D_MODEL), F32),
        scratch_shapes=[pltpu.VMEM((M, 2 * D_MODEL), BF16), pltpu.VMEM((tile, D_FF), BF16)],
        compiler_params=_params(2),
        name="post",
    )(x, o, g, mem, *consts)


def _rot_cols(w):
    half = QK_ROPE_DIM // 2
    return jnp.concatenate([-w[..., half:], w[..., :half]], axis=-1)


def _prep_weights(w_in, w_uq, w_ukv, w_fnet):
    w_cq = w_in[:, :Q_LORA_RANK]
    w_ckv = w_in[:, Q_LORA_RANK:Q_LORA_RANK + KV_LORA_RANK]
    o3 = Q_LORA_RANK + KV_LORA_RANK
    w_kr = w_in[:, o3:o3 + QK_ROPE_DIM]
    w_u = w_in[:, o3 + QK_ROPE_DIM:]
    w_kr_grp = jnp.concatenate([jnp.zeros((D_MODEL, QK_NOPE_DIM), F32), w_kr, _rot_cols(w_kr)], axis=-1)
    w_in_ext = jnp.concatenate([w_cq, w_ckv, w_kr_grp, w_u], axis=-1).astype(BF16)

    w_q_t = w_uq.T.astype(BF16)

    kv_w = w_ukv.reshape(KV_LORA_RANK, MLA_HEADS, QK_NOPE_DIM + V_HEAD_DIM)
    w_uv_t = kv_w[..., QK_NOPE_DIM:].reshape(KV_LORA_RANK, MLA_WIDTH).T.astype(BF16)

    w_f = jnp.zeros((FNET_WIDTH, FNET_WIDTH), F32)
    for gi in range(FNET_GROUPS):
        sl = slice(gi * FNET_GROUP_DIM, (gi + 1) * FNET_GROUP_DIM)
        w_f = w_f.at[sl, sl].set(w_fnet[gi])
    return w_in_ext, w_q_t, w_uv_t, w_f.astype(BF16)


def _dft_mats(n):
    idx = jnp.arange(n, dtype=jnp.int32)
    ang = ((idx[:, None] * idx[None, :]) % n).astype(F32) * (2.0 * jnp.pi / n)
    return jnp.cos(ang), jnp.sin(ang)


def _seq_dft_tables(seq):
    cs_rows, sn_rows = [], []
    n = seq
    for _ in range(DFT_LEVELS):
        ang = jnp.arange(n // 2, dtype=F32) * (2.0 * jnp.pi / n)
        cs_rows.append(jnp.cos(ang))
        sn_rows.append(jnp.sin(ang))
        n //= 2
    rep = lambda rows: jnp.broadcast_to(jnp.concatenate(rows)[:, None], (seq - n, LANES))
    c, s = _dft_mats(n)
    leaf = (jnp.concatenate([c, s], axis=1) * seq ** -0.5).astype(BF16)
    return rep(cs_rows), rep(sn_rows), leaf


def _channel_dft():
    c, s = _dft_mats(FNET_GROUP_DIM)
    return (jnp.concatenate([c, -s], axis=-1) * FNET_GROUP_DIM ** -0.5).astype(BF16)


def _rope_tabs(seq):
    inv = 1.0 / (ROPE_BASE ** (jnp.arange(0, QK_ROPE_DIM, 2, dtype=F32) / QK_ROPE_DIM))
    ang = jnp.arange(seq, dtype=F32)[:, None] * inv[None, :]
    cos, sin = jnp.cos(ang), jnp.sin(ang)
    pad = lambda t: jnp.concatenate([jnp.zeros((seq, QK_NOPE_DIM), F32), t, t,
                                     jnp.zeros((seq, HEAD_PAD - QK_NOPE_DIM - QK_ROPE_DIM), F32)], axis=-1)
    return pad(cos), pad(sin), cos.T, sin.T


def _trunk(x, mem, w, tiles):
    S = x.shape[1]
    ctab, stab, cos_t, sin_t = _rope_tabs(S)
    twc, tws, cs = _seq_dft_tables(S)
    q_t, k, v_t, vr, vi = _proj(x, ctab, stab, cos_t, sin_t, w["ln_mix_g"], w["w_in_ext"], w["q_norm_g"],
                                w["w_q_t"], w["kv_norm_g"], w["w_ukv"], w["w_uv_t"], w["w_cdft"],
                                min(tiles[0], S))
    o = _attn(q_t, k, v_t, min(tiles[1], S))
    g = _dft(vr, vi, twc, tws, cs, w["w_f"])
    return _post(x, o, g, mem, w["w_out"], w["ln_x_g"], w["mem_norm_g"], w["w_xq"], w["w_xkv"], w["w_xo"],
                 w["ln_ffn_g"], w["w_gate_up"], w["w_down"], w["final_norm_g"], min(tiles[2], S))


def kernel(x_prompt, x_sample, mem_prompt, mem_sample, ln_mix_g, w_in, q_norm_g, w_uq, kv_norm_g, w_ukv, w_fnet,
           w_out, ln_x_g, mem_norm_g, w_xq, w_xkv, w_xo, ln_ffn_g, w_gate_up, w_down, final_norm_g):
    assert ln_mix_g.shape[0] == 1, "single-layer trunk"
    w_in_ext, w_q_t, w_uv_t, w_f = _prep_weights(w_in[0], w_uq[0], w_ukv[0], w_fnet[0])
    row = lambda g: g.reshape(1, -1).astype(F32)
    w = dict(
        ln_mix_g=row(ln_mix_g[0]), w_in_ext=w_in_ext, q_norm_g=row(q_norm_g[0]), w_q_t=w_q_t,
        kv_norm_g=row(kv_norm_g[0]), w_ukv=w_ukv[0].astype(BF16), w_uv_t=w_uv_t, w_cdft=_channel_dft(), w_f=w_f,
        w_out=w_out[0].astype(BF16), ln_x_g=row(ln_x_g[0]), mem_norm_g=row(mem_norm_g[0]),
        w_xq=w_xq[0].astype(BF16), w_xkv=w_xkv[0].astype(BF16), w_xo=w_xo[0].astype(BF16),
        ln_ffn_g=row(ln_ffn_g[0]), w_gate_up=w_gate_up[0].astype(BF16), w_down=w_down[0].astype(BF16),
        final_norm_g=row(final_norm_g),
    )
    return (_trunk(x_prompt, mem_prompt, w, TILES), _trunk(x_sample, mem_sample, w, TILES))
```

```python
import functools

import jax
import jax.numpy as jnp
from jax import lax
from jax.experimental import pallas as pl
from jax.experimental.pallas import tpu as pltpu

F32 = jnp.float32
BF16 = jnp.bfloat16

D_MODEL = 1024
MLA_HEADS = 8
QK_NOPE_DIM = 64
QK_ROPE_DIM = 32
V_HEAD_DIM = 64
Q_LORA_RANK = 384
KV_LORA_RANK = 256
FNET_GROUPS = 4
FNET_GROUP_DIM = 128
FNET_WIDTH = FNET_GROUPS * FNET_GROUP_DIM
MLA_WIDTH = MLA_HEADS * V_HEAD_DIM
XATTN_HEADS = 4
XATTN_HEAD_DIM = D_MODEL // XATTN_HEADS
D_FF = 2816
ROPE_BASE = 10000.0
NORM_EPS = 1e-6
LOG2_E = 1.4426950408889634

LANES = 128
HEAD_PAD = LANES
FF_CHUNK = 256
VMEM_LIMIT = 56 * 1024 * 1024
TILES = (512, 512, 512)
ATTN_KEY_CHUNK = 1024
ATTN_PILOT_KEYS = 64
ATTN_SHIFT_MARGIN = 30.0
ATTN_SAFE_GAP = 90.0
ATTN_NORM_SLACK = 1.02
DFT_LEVELS = 3
DFT_ROW_CHUNK = 256

_O_CQ = 0
_O_CKV = _O_CQ + Q_LORA_RANK
_O_KR = _O_CKV + KV_LORA_RANK
_O_U = _O_KR + HEAD_PAD
IN_EXT = _O_U + FNET_WIDTH


def _rms(x, g):
    return x * lax.rsqrt(jnp.mean(x * x, axis=-1, keepdims=True) + NORM_EPS) * g


def _dot(a, b):
    return jnp.dot(a, b, preferred_element_type=F32)


def _dot_nt(a, b):
    return lax.dot_general(a, b, (((1,), (1,)), ((), ())), preferred_element_type=F32)


def _const_spec(shape):
    zeros = (0,) * len(shape)
    return pl.BlockSpec(shape, lambda *_: zeros, pipeline_mode=pl.Buffered(1))


def _params(n_axes, flags=None):
    return pltpu.CompilerParams(dimension_semantics=("arbitrary",) * n_axes,
                                vmem_limit_bytes=VMEM_LIMIT, flags=flags)


def _proj_kernel(x_ref, ctab_ref, stab_ref, cos_t_ref, sin_t_ref, ln_g_ref, w_in_ref, qn_g_ref, w_q_t_ref,
                 kvn_g_ref, w_ukv_ref, w_uv_t_ref, w_cdft_ref, head_sel_ref,
                 q_t_ref, k_ref, v_t_ref, kn2_ref, qn2_ref, lb_ref, vr_ref, vi_ref):
    scale = (QK_NOPE_DIM + QK_ROPE_DIM) ** -0.5 * LOG2_E
    half = QK_ROPE_DIM // 2
    qk_dim = QK_NOPE_DIM + QK_ROPE_DIM
    h = _rms(x_ref[0], ln_g_ref[...]).astype(BF16)
    z = _dot(h, w_in_ref[...])

    cq = _rms(z[:, _O_CQ:_O_CQ + Q_LORA_RANK], qn_g_ref[...]).astype(BF16)
    q_t = _dot_nt(w_q_t_ref[...], cq)
    cos_t = cos_t_ref[...] * scale
    sin_t = sin_t_ref[...] * scale
    zero_rows = jnp.zeros((HEAD_PAD - qk_dim, q_t.shape[1]), BF16)
    q_heads = []
    for hd in range(MLA_HEADS):
        src, dst = hd * qk_dim, hd * HEAD_PAD
        x1 = q_t[src + QK_NOPE_DIM:src + QK_NOPE_DIM + half, :]
        x2 = q_t[src + QK_NOPE_DIM + half:src + qk_dim, :]
        q_h = jnp.concatenate([q_t[src:src + QK_NOPE_DIM, :] * scale, x1 * cos_t - x2 * sin_t,
                               x2 * cos_t + x1 * sin_t], axis=0)
        q_heads.append(jnp.concatenate([q_h.astype(BF16), zero_rows], axis=0))
        q_t_ref[0, dst:dst + HEAD_PAD, :] = q_heads[hd]
        qn2_ref[0, hd:hd + 1, :] = jnp.sum(q_h * q_h, axis=0, keepdims=True)

    ckv = _rms(z[:, _O_CKV:_O_CKV + KV_LORA_RANK], kvn_g_ref[...]).astype(BF16)
    kv = _dot(ckv, w_ukv_ref[...])
    zk = z[:, _O_KR:_O_KR + HEAD_PAD]
    kr = zk * ctab_ref[...] + pltpu.roll(zk, HEAD_PAD - QK_ROPE_DIM, 1) * stab_ref[...]
    nope = lax.broadcasted_iota(jnp.int32, kr.shape, 1) < QK_NOPE_DIM
    k_heads = [jnp.where(nope, kv[:, hd * HEAD_PAD:(hd + 1) * HEAD_PAD], kr) for hd in range(MLA_HEADS)]
    for hd in range(MLA_HEADS):
        k_ref[0, :, hd * HEAD_PAD:(hd + 1) * HEAD_PAD] = k_heads[hd].astype(BF16)
    k_all = jnp.concatenate(k_heads, axis=1)
    kn2_ref[0] = _dot((k_all * k_all).astype(BF16), head_sel_ref[...])

    pilot_k = k_all[:ATTN_PILOT_KEYS, :].astype(BF16)
    for hd in range(MLA_HEADS):
        pilot = _dot(pilot_k[:, hd * HEAD_PAD:(hd + 1) * HEAD_PAD], q_heads[hd])
        lb_ref[0, hd:hd + 1, :] = jnp.max(pilot, axis=0, keepdims=True)
    v_t_ref[0] = _dot_nt(w_uv_t_ref[...], ckv).astype(BF16)

    for gi in range(FNET_GROUPS):
        sl = slice(gi * FNET_GROUP_DIM, (gi + 1) * FNET_GROUP_DIM)
        v = _dot(z[:, _O_U + gi * FNET_GROUP_DIM:_O_U + (gi + 1) * FNET_GROUP_DIM].astype(BF16), w_cdft_ref[...])
        vr_ref[0, :, sl] = v[:, :FNET_GROUP_DIM].astype(BF16)
        vi_ref[0, :, sl] = v[:, FNET_GROUP_DIM:].astype(BF16)


def _proj(x, ctab, stab, cos_t, sin_t, ln_g, w_in_ext, qn_g, w_q_t, kvn_g, w_ukv, w_uv_t, w_cdft, tile):
    B, S, _ = x.shape
    hp = MLA_HEADS * HEAD_PAD
    tok = lambda w: pl.BlockSpec((1, tile, w), lambda b, t: (b, t, 0))
    tok_t = lambda w: pl.BlockSpec((1, w, tile), lambda b, t: (b, 0, t))
    tab = pl.BlockSpec((tile, LANES), lambda b, t: (t, 0))
    tab_t = pl.BlockSpec((cos_t.shape[0], tile), lambda b, t: (0, t))
    head_sel = (jnp.arange(hp)[:, None] // HEAD_PAD == jnp.arange(LANES)[None, :]).astype(BF16)
    return pl.pallas_call(
        _proj_kernel,
        grid=(B, S // tile),
        in_specs=[tok(D_MODEL), tab, tab, tab_t, tab_t, _const_spec(ln_g.shape), _const_spec(w_in_ext.shape),
                  _const_spec(qn_g.shape), _const_spec(w_q_t.shape), _const_spec(kvn_g.shape),
                  _const_spec(w_ukv.shape), _const_spec(w_uv_t.shape), _const_spec(w_cdft.shape),
                  _const_spec(head_sel.shape)],
        out_specs=[tok_t(hp), tok(hp), tok_t(MLA_WIDTH), tok(LANES), tok_t(MLA_HEADS), tok_t(MLA_HEADS),
                   tok(FNET_WIDTH), tok(FNET_WIDTH)],
        out_shape=[jax.ShapeDtypeStruct((B, hp, S), BF16), jax.ShapeDtypeStruct((B, S, hp), BF16),
                   jax.ShapeDtypeStruct((B, MLA_WIDTH, S), BF16), jax.ShapeDtypeStruct((B, S, LANES), F32),
                   jax.ShapeDtypeStruct((B, MLA_HEADS, S), F32), jax.ShapeDtypeStruct((B, MLA_HEADS, S), F32)]
                  + [jax.ShapeDtypeStruct((B, S, FNET_WIDTH), BF16)] * 2,
        compiler_params=_params(2),
        name="proj",
    )(x, ctab, stab, cos_t, sin_t, ln_g, w_in_ext, qn_g, w_q_t, kvn_g, w_ukv, w_uv_t, w_cdft, head_sel)


def _attn_kernel(safe_ref, q_t_ref, k_ref, v_t_ref, shift_ref, o_ref, s_scr):
    S = k_ref.shape[1]
    kc = min(ATTN_KEY_CHUNK, S)
    n_chunks = S // kc
    heads = range(MLA_HEADS)

    def k_blk(hh, r0, r1):
        return k_ref[0, r0:r1, hh * HEAD_PAD:(hh + 1) * HEAD_PAD]

    def q_blk(hh):
        return q_t_ref[0, hh * HEAD_PAD:(hh + 1) * HEAD_PAD, :]

    def v_blk(hh, c):
        return v_t_ref[0, hh * V_HEAD_DIM:(hh + 1) * V_HEAD_DIM, c * kc:(c + 1) * kc]

    def finish(acc, den):
        o_ref[0] = jnp.concatenate([a / l for a, l in zip(acc, den)], axis=0).T.astype(BF16)

    all_safe = safe_ref[pl.program_id(0), pl.program_id(1)] != 0

    @pl.when(all_safe)
    def _single_pass():
        acc, den = [], []
        for hh in heads:
            a, l = None, None
            for c in range(n_chunks):
                p = jnp.exp2(_dot(k_blk(hh, c * kc, (c + 1) * kc), q_blk(hh)) - shift_ref[0, hh:hh + 1, :])
                o_c, l_c = _dot(v_blk(hh, c), p.astype(BF16)), jnp.sum(p, axis=0, keepdims=True)
                a, l = (o_c, l_c) if a is None else (a + o_c, l + l_c)
            acc.append(a)
            den.append(l)
        finish(acc, den)

    @pl.when(jnp.logical_not(all_safe))
    def _two_pass():
        def score_chunk(hh, c):
            s = _dot(k_blk(hh, c * kc, (c + 1) * kc), q_blk(hh))
            s_scr[hh % 2, c * kc:(c + 1) * kc, :] = s
            return jnp.max(s, axis=0, keepdims=True)

        items = [(hh, c) for hh in heads for c in range(n_chunks)]
        m_parts = [[] for _ in heads]
        m_head, acc, den = [None] * len(heads), [None] * len(heads), [None] * len(heads)
        for i in range(len(items) + n_chunks):
            if i < len(items):
                hh, c = items[i]
                m_parts[hh].append(score_chunk(hh, c))
                if c == n_chunks - 1:
                    m_head[hh] = functools.reduce(jnp.maximum, m_parts[hh])
            if i >= n_chunks:
                hh, c = items[i - n_chunks]
                p = jnp.exp2(s_scr[hh % 2, c * kc:(c + 1) * kc, :] - m_head[hh])
                o_c, l_c = _dot(v_blk(hh, c), p.astype(BF16)), jnp.sum(p, axis=0, keepdims=True)
                acc[hh] = o_c if acc[hh] is None else acc[hh] + o_c
                den[hh] = l_c if den[hh] is None else den[hh] + l_c
        finish(acc, den)


def _score_shifts(kn2, qn2, lb, tile):
    B, _, S = lb.shape
    kmax2 = jnp.max(kn2[:, :, :MLA_HEADS], axis=1)
    ub = jnp.sqrt(qn2 * kmax2[:, :, None]) * ATTN_NORM_SLACK
    ok = (ub - lb) <= ATTN_SAFE_GAP
    safe = jnp.all(ok.reshape(B, MLA_HEADS, S // tile, tile), axis=(1, 3))
    return lb + ATTN_SHIFT_MARGIN, safe.astype(jnp.int32)


def _attn(q_t, k, v_t, shift, safe, tile):
    B, S, _ = k.shape
    hp = MLA_HEADS * HEAD_PAD
    return pl.pallas_call(
        _attn_kernel,
        grid_spec=pltpu.PrefetchScalarGridSpec(
            num_scalar_prefetch=1,
            grid=(B, S // tile),
            in_specs=[pl.BlockSpec((1, hp, tile), lambda b, t, safe: (b, 0, t)),
                      pl.BlockSpec((1, S, hp), lambda b, t, safe: (b, 0, 0)),
                      pl.BlockSpec((1, MLA_WIDTH, S), lambda b, t, safe: (b, 0, 0)),
                      pl.BlockSpec((1, MLA_HEADS, tile), lambda b, t, safe: (b, 0, t))],
            out_specs=pl.BlockSpec((1, tile, MLA_WIDTH), lambda b, t, safe: (b, t, 0)),
            scratch_shapes=[pltpu.VMEM((2, S, tile), F32)]),
        out_shape=jax.ShapeDtypeStruct((B, S, MLA_WIDTH), BF16),
        compiler_params=_params(2),
        name="attn",
    )(safe, q_t, k, v_t, shift)


def _bit_reverse(i, bits):
    return int(format(i, "0%db" % bits)[::-1], 2)


def _dft_kernel(vr_ref, vi_ref, twc_ref, tws_ref, cs_ref, w_f_ref, g_ref, xr_scr, xi_scr, xb_scr, g_scr):
    S = vr_ref.shape[1]
    n_leaves = 2 ** DFT_LEVELS
    leaf = S // n_leaves
    chunk = min(DFT_ROW_CHUNK, leaf)
    reps = FNET_WIDTH // LANES
    n, off = S, 0
    for lev in range(DFT_LEVELS):
        half = n // 2
        last = lev == DFT_LEVELS - 1
        for base in range(0, S, n):
            for c0 in range(0, half, chunk):
                top = slice(base + c0, base + c0 + chunk)
                bot = slice(base + half + c0, base + half + c0 + chunk)
                if lev == 0:
                    tr, ti = vr_ref[0, top, :].astype(F32), vi_ref[0, top, :].astype(F32)
                    br, bi = vr_ref[0, bot, :].astype(F32), vi_ref[0, bot, :].astype(F32)
                else:
                    tr, ti, br, bi = xr_scr[top, :], xi_scr[top, :], xr_scr[bot, :], xi_scr[bot, :]
                c = jnp.concatenate([twc_ref[off + c0:off + c0 + chunk, :]] * reps, axis=1)
                s = jnp.concatenate([tws_ref[off + c0:off + c0 + chunk, :]] * reps, axis=1)
                dr, di = tr - br, ti - bi
                ar, ai = tr + br, ti + bi
                mr, mi = dr * c + di * s, di * c - dr * s
                if last:
                    blk, r0 = (base // n) * 2, c0
                    xb_scr[blk, r0:r0 + chunk, :] = ar.astype(BF16)
                    xb_scr[blk, leaf + r0:leaf + r0 + chunk, :] = ai.astype(BF16)
                    xb_scr[blk + 1, r0:r0 + chunk, :] = mr.astype(BF16)
                    xb_scr[blk + 1, leaf + r0:leaf + r0 + chunk, :] = mi.astype(BF16)
                else:
                    xr_scr[top, :], xi_scr[top, :] = ar, ai
                    xr_scr[bot, :], xi_scr[bot, :] = mr, mi
        off += half
        n = half
    for blk in range(n_leaves):
        f = _dot(cs_ref[...], xb_scr[blk])
        g = _dot(f.astype(BF16), w_f_ref[...])
        r = _bit_reverse(blk, DFT_LEVELS)
        for lt in range(reps):
            g_scr[lt, pl.ds(r, leaf, stride=n_leaves), :] = g[:, lt * LANES:(lt + 1) * LANES]
    for lt in range(reps):
        g_ref[0, :, lt * LANES:(lt + 1) * LANES] = g_scr[lt].astype(BF16)


def _dft(vr, vi, twc, tws, cs, w_f):
    B, S, _ = vr.shape
    n_leaves = 2 ** DFT_LEVELS
    leaf = S // n_leaves
    bat = pl.BlockSpec((1, S, FNET_WIDTH), lambda b: (b, 0, 0))
    return pl.pallas_call(
        _dft_kernel,
        grid=(B,),
        in_specs=[bat, bat, _const_spec(twc.shape), _const_spec(tws.shape), _const_spec(cs.shape),
                  _const_spec(w_f.shape)],
        out_specs=bat,
        out_shape=jax.ShapeDtypeStruct((B, S, FNET_WIDTH), BF16),
        scratch_shapes=[pltpu.VMEM((S, FNET_WIDTH), F32), pltpu.VMEM((S, FNET_WIDTH), F32),
                        pltpu.VMEM((n_leaves, 2 * leaf, FNET_WIDTH), BF16),
                        pltpu.VMEM((FNET_WIDTH // LANES, S, LANES), F32)],
        compiler_params=_params(1),
        name="dft",
    )(vr, vi, twc, tws, cs, w_f)


def _post_kernel(x_ref, o_ref, g_ref, mem_ref, w_out_ref, ln_x_g_ref, mem_g_ref, w_xq_ref, w_xkv_ref,
                 w_xo_ref, ln_f_g_ref, w_gu_ref, w_d_ref, fin_g_ref, y_ref, kv_scr, a_scr):
    @pl.when(pl.program_id(1) == 0)
    def _():
        m = _rms(mem_ref[0], mem_g_ref[...]).astype(BF16)
        kv_scr[...] = _dot(m, w_xkv_ref[...]).astype(BF16)

    x1 = (x_ref[0] + _dot(o_ref[0], w_out_ref[:MLA_WIDTH, :])
          + _dot(g_ref[0], w_out_ref[MLA_WIDTH:, :]))

    hq = _rms(x1, ln_x_g_ref[...]).astype(BF16)
    q = (_dot(hq, w_xq_ref[...]) * (XATTN_HEAD_DIM ** -0.5)).astype(BF16)
    heads = []
    for hd in range(XATTN_HEADS):
        sl = slice(hd * XATTN_HEAD_DIM, (hd + 1) * XATTN_HEAD_DIM)
        s = _dot_nt(q[:, sl], kv_scr[:, sl])
        m = jnp.max(s, axis=-1, keepdims=True)
        p = jnp.exp(s - m)
        l = jnp.sum(p, axis=-1, keepdims=True)
        vh = kv_scr[:, D_MODEL + hd * XATTN_HEAD_DIM:D_MODEL + (hd + 1) * XATTN_HEAD_DIM]
        heads.append((_dot(p.astype(BF16), vh) / l).astype(BF16))
    x2 = x1 + _dot(jnp.concatenate(heads, axis=-1), w_xo_ref[...])

    hf = _rms(x2, ln_f_g_ref[...]).astype(BF16)
    for c in range(D_FF // FF_CHUNK):
        sl = slice(c * FF_CHUNK, (c + 1) * FF_CHUNK)
        gate = _dot(hf, w_gu_ref[:, sl])
        up = _dot(hf, w_gu_ref[:, D_FF + c * FF_CHUNK:D_FF + (c + 1) * FF_CHUNK])
        a_scr[:, sl] = (gate * jax.nn.sigmoid(gate) * up).astype(BF16)
    x3 = x2 + _dot(a_scr[...], w_d_ref[...])
    y_ref[0] = _rms(x3, fin_g_ref[...])


def _post(x, o, g, mem, w_out, ln_x_g, mem_g, w_xq, w_xkv, w_xo, ln_f_g, w_gu, w_d, fin_g, tile):
    B, S, _ = x.shape
    M = mem.shape[1]
    tok = lambda w: pl.BlockSpec((1, tile, w), lambda b, t: (b, t, 0))
    consts = [w_out, ln_x_g, mem_g, w_xq, w_xkv, w_xo, ln_f_g, w_gu, w_d, fin_g]
    return pl.pallas_call(
        _post_kernel,
        grid=(B, S // tile),
        in_specs=[tok(D_MODEL), tok(MLA_WIDTH), tok(FNET_WIDTH),
                  pl.BlockSpec((1, M, D_MODEL), lambda b, t: (b, 0, 0))]
                 + [_const_spec(c.shape) for c in consts],
        out_specs=tok(D_MODEL),
        out_shape=jax.ShapeDtypeStruct((B, S, D_MODEL), F32),
        scratch_shapes=[pltpu.VMEM((M, 2 * D_MODEL), BF16), pltpu.VMEM((tile, D_FF), BF16)],
        compiler_params=_params(2),
        name="post",
    )(x, o, g, mem, *consts)


def _rot_cols(w):
    half = QK_ROPE_DIM // 2
    return jnp.concatenate([-w[..., half:], w[..., :half]], axis=-1)


def _prep_weights(w_in, w_uq, w_ukv, w_fnet):
    w_cq = w_in[:, :Q_LORA_RANK]
    w_ckv = w_in[:, Q_LORA_RANK:Q_LORA_RANK + KV_LORA_RANK]
    o3 = Q_LORA_RANK + KV_LORA_RANK
    w_kr = w_in[:, o3:o3 + QK_ROPE_DIM]
    w_u = w_in[:, o3 + QK_ROPE_DIM:]
    w_kr_grp = jnp.concatenate([jnp.zeros((D_MODEL, QK_NOPE_DIM), F32), w_kr, _rot_cols(w_kr)], axis=-1)
    w_in_ext = jnp.concatenate([w_cq, w_ckv, w_kr_grp, w_u], axis=-1).astype(BF16)

    w_q_t = w_uq.T.astype(BF16)

    kv_w = w_ukv.reshape(KV_LORA_RANK, MLA_HEADS, QK_NOPE_DIM + V_HEAD_DIM)
    w_uv_t = kv_w[..., QK_NOPE_DIM:].reshape(KV_LORA_RANK, MLA_WIDTH).T.astype(BF16)

    w_f = jnp.zeros((FNET_WIDTH, FNET_WIDTH), F32)
    for gi in range(FNET_GROUPS):
        sl = slice(gi * FNET_GROUP_DIM, (gi + 1) * FNET_GROUP_DIM)
        w_f = w_f.at[sl, sl].set(w_fnet[gi])
    return w_in_ext, w_q_t, w_uv_t, w_f.astype(BF16)


def _dft_mats(n):
    idx = jnp.arange(n, dtype=jnp.int32)
    ang = ((idx[:, None] * idx[None, :]) % n).astype(F32) * (2.0 * jnp.pi / n)
    return jnp.cos(ang), jnp.sin(ang)


def _seq_dft_tables(seq):
    cs_rows, sn_rows = [], []
    n = seq
    for _ in range(DFT_LEVELS):
        ang = jnp.arange(n // 2, dtype=F32) * (2.0 * jnp.pi / n)
        cs_rows.append(jnp.cos(ang))
        sn_rows.append(jnp.sin(ang))
        n //= 2
    rep = lambda rows: jnp.broadcast_to(jnp.concatenate(rows)[:, None], (seq - n, LANES))
    c, s = _dft_mats(n)
    leaf = (jnp.concatenate([c, s], axis=1) * seq ** -0.5).astype(BF16)
    return rep(cs_rows), rep(sn_rows), leaf


def _channel_dft():
    c, s = _dft_mats(FNET_GROUP_DIM)
    return (jnp.concatenate([c, -s], axis=-1) * FNET_GROUP_DIM ** -0.5).astype(BF16)


def _rope_tabs(seq):
    inv = 1.0 / (ROPE_BASE ** (jnp.arange(0, QK_ROPE_DIM, 2, dtype=F32) / QK_ROPE_DIM))
    ang = jnp.arange(seq, dtype=F32)[:, None] * inv[None, :]
    cos, sin = jnp.cos(ang), jnp.sin(ang)
    pad = lambda t: jnp.concatenate([jnp.zeros((seq, QK_NOPE_DIM), F32), t, t,
                                     jnp.zeros((seq, HEAD_PAD - QK_NOPE_DIM - QK_ROPE_DIM), F32)], axis=-1)
    return pad(cos), pad(sin), cos.T, sin.T


def _trunk(x, mem, w, tiles):
    S = x.shape[1]
    ctab, stab, cos_t, sin_t = _rope_tabs(S)
    twc, tws, cs = _seq_dft_tables(S)
    q_t, k, v_t, kn2, qn2, lb, vr, vi = _proj(x, ctab, stab, cos_t, sin_t, w["ln_mix_g"], w["w_in_ext"],
                                              w["q_norm_g"], w["w_q_t"], w["kv_norm_g"], w["w_ukv"],
                                              w["w_uv_t"], w["w_cdft"], min(tiles[0], S))
    shift, safe = _score_shifts(kn2, qn2, lb, min(tiles[1], S))
    o = _attn(q_t, k, v_t, shift, safe, min(tiles[1], S))
    g = _dft(vr, vi, twc, tws, cs, w["w_f"])
    return _post(x, o, g, mem, w["w_out"], w["ln_x_g"], w["mem_norm_g"], w["w_xq"], w["w_xkv"], w["w_xo"],
                 w["ln_ffn_g"], w["w_gate_up"], w["w_down"], w["final_norm_g"], min(tiles[2], S))


def kernel(x_prompt, x_sample, mem_prompt, mem_sample, ln_mix_g, w_in, q_norm_g, w_uq, kv_norm_g, w_ukv, w_fnet,
           w_out, ln_x_g, mem_norm_g, w_xq, w_xkv, w_xo, ln_ffn_g, w_gate_up, w_down, final_norm_g):
    assert ln_mix_g.shape[0] == 1, "single-layer trunk"
    w_in_ext, w_q_t, w_uv_t, w_f = _prep_weights(w_in[0], w_uq[0], w_ukv[0], w_fnet[0])
    row = lambda g: g.reshape(1, -1).astype(F32)
    w = dict(
        ln_mix_g=row(ln_mix_g[0]), w_in_ext=w_in_ext, q_norm_g=row(q_norm_g[0]), w_q_t=w_q_t,
        kv_norm_g=row(kv_norm_g[0]), w_ukv=w_ukv[0].astype(BF16), w_uv_t=w_uv_t, w_cdft=_channel_dft(), w_f=w_f,
        w_out=w_out[0].astype(BF16), ln_x_g=row(ln_x_g[0]), mem_norm_g=row(mem_norm_g[0]),
        w_xq=w_xq[0].astype(BF16), w_xkv=w_xkv[0].astype(BF16), w_xo=w_xo[0].astype(BF16),
        ln_ffn_g=row(ln_ffn_g[0]), w_gate_up=w_gate_up[0].astype(BF16), w_down=w_down[0].astype(BF16),
        final_norm_g=row(final_norm_g),
    )
    return (_trunk(x_prompt, mem_prompt, w, TILES), _trunk(x_sample, mem_sample, w, TILES))
```

```python
import functools

import numpy as np
import jax
import jax.numpy as jnp
from jax import lax
from jax.experimental import pallas as pl
from jax.experimental.pallas import tpu as pltpu

F32 = jnp.float32
BF16 = jnp.bfloat16

D_MODEL = 1024
MLA_HEADS = 8
QK_NOPE_DIM = 64
QK_ROPE_DIM = 32
V_HEAD_DIM = 64
Q_LORA_RANK = 384
KV_LORA_RANK = 256
FNET_GROUPS = 4
FNET_GROUP_DIM = 128
FNET_WIDTH = FNET_GROUPS * FNET_GROUP_DIM
MLA_WIDTH = MLA_HEADS * V_HEAD_DIM
XATTN_HEADS = 4
XATTN_HEAD_DIM = D_MODEL // XATTN_HEADS
D_FF = 2816
ROPE_BASE = 10000.0
NORM_EPS = 1e-6
LOG2_E = 1.4426950408889634

LANES = 128
HEAD_PAD = LANES
FF_CHUNK = 256
VMEM_LIMIT = 56 * 1024 * 1024
TILES = (512, 512, 512)
ATTN_KEY_CHUNK = 1024
ATTN_PILOT_KEYS = 64
ATTN_SHIFT_MARGIN = 30.0
ATTN_SAFE_GAP = 90.0
ATTN_NORM_SLACK = 1.02
DFT_LEVELS = 3
DFT_ROW_CHUNK = 256

_O_CQ = 0
_O_CKV = _O_CQ + Q_LORA_RANK
_O_KR = _O_CKV + KV_LORA_RANK
_O_U = _O_KR + HEAD_PAD
IN_EXT = _O_U + FNET_WIDTH


def _rms(x, g):
    return x * lax.rsqrt(jnp.mean(x * x, axis=-1, keepdims=True) + NORM_EPS) * g


def _dot(a, b):
    return jnp.dot(a, b, preferred_element_type=F32)


def _dot_nt(a, b):
    return lax.dot_general(a, b, (((1,), (1,)), ((), ())), preferred_element_type=F32)


def _const_spec(shape):
    zeros = (0,) * len(shape)
    return pl.BlockSpec(shape, lambda *_: zeros, pipeline_mode=pl.Buffered(1))


def _params(n_axes, flags=None):
    return pltpu.CompilerParams(dimension_semantics=("arbitrary",) * n_axes,
                                vmem_limit_bytes=VMEM_LIMIT, flags=flags)


def _proj_kernel(x_ref, ctab_ref, stab_ref, cos_t_ref, sin_t_ref, ln_g_ref, w_in_ref, qn_g_ref, w_q_t_ref,
                 kvn_g_ref, w_ukv_ref, w_uv_t_ref, w_cdft_ref, head_sel_ref,
                 q_t_ref, k_ref, v_t_ref, kn2_ref, qn2_ref, lb_ref, vr_ref, vi_ref):
    scale = (QK_NOPE_DIM + QK_ROPE_DIM) ** -0.5 * LOG2_E
    half = QK_ROPE_DIM // 2
    qk_dim = QK_NOPE_DIM + QK_ROPE_DIM
    h = _rms(x_ref[0], ln_g_ref[...]).astype(BF16)
    z = _dot(h, w_in_ref[...])

    cq = _rms(z[:, _O_CQ:_O_CQ + Q_LORA_RANK], qn_g_ref[...]).astype(BF16)
    q_t = _dot_nt(w_q_t_ref[...], cq)
    cos_t = cos_t_ref[...] * scale
    sin_t = sin_t_ref[...] * scale
    zero_rows = jnp.zeros((HEAD_PAD - qk_dim, q_t.shape[1]), BF16)
    q_heads = []
    for hd in range(MLA_HEADS):
        src, dst = hd * qk_dim, hd * HEAD_PAD
        x1 = q_t[src + QK_NOPE_DIM:src + QK_NOPE_DIM + half, :]
        x2 = q_t[src + QK_NOPE_DIM + half:src + qk_dim, :]
        q_h = jnp.concatenate([q_t[src:src + QK_NOPE_DIM, :] * scale, x1 * cos_t - x2 * sin_t,
                               x2 * cos_t + x1 * sin_t], axis=0)
        q_heads.append(jnp.concatenate([q_h.astype(BF16), zero_rows], axis=0))
        q_t_ref[0, dst:dst + HEAD_PAD, :] = q_heads[hd]
        qn2_ref[0, hd:hd + 1, :] = jnp.sum(q_h * q_h, axis=0, keepdims=True)

    ckv = _rms(z[:, _O_CKV:_O_CKV + KV_LORA_RANK], kvn_g_ref[...]).astype(BF16)
    kv = _dot(ckv, w_ukv_ref[...])
    zk = z[:, _O_KR:_O_KR + HEAD_PAD]
    kr = zk * ctab_ref[...] + pltpu.roll(zk, HEAD_PAD - QK_ROPE_DIM, 1) * stab_ref[...]
    nope = lax.broadcasted_iota(jnp.int32, kr.shape, 1) < QK_NOPE_DIM
    k_heads = [jnp.where(nope, kv[:, hd * HEAD_PAD:(hd + 1) * HEAD_PAD], kr) for hd in range(MLA_HEADS)]
    for hd in range(MLA_HEADS):
        k_ref[0, hd] = k_heads[hd].astype(BF16)
    k_all = jnp.concatenate(k_heads, axis=1)
    kn2_ref[0] = _dot((k_all * k_all).astype(BF16), head_sel_ref[...])

    pilot_k = k_all[:ATTN_PILOT_KEYS, :].astype(BF16)
    for hd in range(MLA_HEADS):
        pilot = _dot(pilot_k[:, hd * HEAD_PAD:(hd + 1) * HEAD_PAD], q_heads[hd])
        lb_ref[0, hd:hd + 1, :] = jnp.max(pilot, axis=0, keepdims=True)
    v_t_ref[0] = _dot_nt(w_uv_t_ref[...], ckv).astype(BF16)

    for gi in range(FNET_GROUPS):
        sl = slice(gi * FNET_GROUP_DIM, (gi + 1) * FNET_GROUP_DIM)
        v = _dot(z[:, _O_U + gi * FNET_GROUP_DIM:_O_U + (gi + 1) * FNET_GROUP_DIM].astype(BF16), w_cdft_ref[...])
        vr_ref[0, :, sl] = v[:, :FNET_GROUP_DIM].astype(BF16)
        vi_ref[0, :, sl] = v[:, FNET_GROUP_DIM:].astype(BF16)


def _proj(x, ctab, stab, cos_t, sin_t, ln_g, w_in_ext, qn_g, w_q_t, kvn_g, w_ukv, w_uv_t, w_cdft, tile):
    B, S, _ = x.shape
    hp = MLA_HEADS * HEAD_PAD
    tok = lambda w: pl.BlockSpec((1, tile, w), lambda b, t: (b, t, 0))
    tok_t = lambda w: pl.BlockSpec((1, w, tile), lambda b, t: (b, 0, t))
    tab = pl.BlockSpec((tile, LANES), lambda b, t: (t, 0))
    tab_t = pl.BlockSpec((cos_t.shape[0], tile), lambda b, t: (0, t))
    head_sel = jnp.asarray(np.arange(hp)[:, None] // HEAD_PAD == np.arange(LANES)[None, :], BF16)
    return pl.pallas_call(
        _proj_kernel,
        grid=(B, S // tile),
        in_specs=[tok(D_MODEL), tab, tab, tab_t, tab_t, _const_spec(ln_g.shape), _const_spec(w_in_ext.shape),
                  _const_spec(qn_g.shape), _const_spec(w_q_t.shape), _const_spec(kvn_g.shape),
                  _const_spec(w_ukv.shape), _const_spec(w_uv_t.shape), _const_spec(w_cdft.shape),
                  _const_spec(head_sel.shape)],
        out_specs=[tok_t(hp), pl.BlockSpec((1, MLA_HEADS, tile, HEAD_PAD), lambda b, t: (b, 0, t, 0)),
                   tok_t(MLA_WIDTH), tok(LANES), tok_t(MLA_HEADS), tok_t(MLA_HEADS),
                   tok(FNET_WIDTH), tok(FNET_WIDTH)],
        out_shape=[jax.ShapeDtypeStruct((B, hp, S), BF16), jax.ShapeDtypeStruct((B, MLA_HEADS, S, HEAD_PAD), BF16),
                   jax.ShapeDtypeStruct((B, MLA_WIDTH, S), BF16), jax.ShapeDtypeStruct((B, S, LANES), F32),
                   jax.ShapeDtypeStruct((B, MLA_HEADS, S), F32), jax.ShapeDtypeStruct((B, MLA_HEADS, S), F32)]
                  + [jax.ShapeDtypeStruct((B, S, FNET_WIDTH), BF16)] * 2,
        compiler_params=_params(2),
        name="proj",
    )(x, ctab, stab, cos_t, sin_t, ln_g, w_in_ext, qn_g, w_q_t, kvn_g, w_ukv, w_uv_t, w_cdft, head_sel)


def _attn_kernel(safe_ref, q_t_ref, k_ref, v_t_ref, shift_ref, o_ref, s_scr):
    S = k_ref.shape[2]
    kc = min(ATTN_KEY_CHUNK, S)
    n_chunks = S // kc
    heads = range(MLA_HEADS)

    def k_blk(hh, r0, r1):
        return k_ref[0, hh, r0:r1, :]

    def q_blk(hh):
        return q_t_ref[0, hh * HEAD_PAD:(hh + 1) * HEAD_PAD, :]

    def v_blk(hh, c):
        return v_t_ref[0, hh * V_HEAD_DIM:(hh + 1) * V_HEAD_DIM, c * kc:(c + 1) * kc]

    def finish(acc, den):
        o_ref[0] = jnp.concatenate([a / l for a, l in zip(acc, den)], axis=0).T.astype(BF16)

    all_safe = safe_ref[pl.program_id(0), pl.program_id(1)] != 0

    @pl.when(all_safe)
    def _single_pass():
        items = [(hh, c) for hh in heads for c in range(n_chunks)]
        score = lambda hh, c: _dot(k_blk(hh, c * kc, (c + 1) * kc), q_blk(hh))
        acc, den = [None] * len(heads), [None] * len(heads)
        s_next = score(*items[0])
        for i, (hh, c) in enumerate(items):
            s = s_next
            if i + 1 < len(items):
                s_next = score(*items[i + 1])
            p = jnp.exp2(s - shift_ref[0, hh:hh + 1, :])
            o_c, l_c = _dot(v_blk(hh, c), p.astype(BF16)), jnp.sum(p, axis=0, keepdims=True)
            acc[hh] = o_c if acc[hh] is None else acc[hh] + o_c
            den[hh] = l_c if den[hh] is None else den[hh] + l_c
        finish(acc, den)

    @pl.when(jnp.logical_not(all_safe))
    def _two_pass():
        def score_chunk(hh, c):
            s = _dot(k_blk(hh, c * kc, (c + 1) * kc), q_blk(hh))
            s_scr[hh % 2, c * kc:(c + 1) * kc, :] = s
            return jnp.max(s, axis=0, keepdims=True)

        items = [(hh, c) for hh in heads for c in range(n_chunks)]
        m_parts = [[] for _ in heads]
        m_head, acc, den = [None] * len(heads), [None] * len(heads), [None] * len(heads)
        for i in range(len(items) + n_chunks):
            if i < len(items):
                hh, c = items[i]
                m_parts[hh].append(score_chunk(hh, c))
                if c == n_chunks - 1:
                    m_head[hh] = functools.reduce(jnp.maximum, m_parts[hh])
            if i >= n_chunks:
                hh, c = items[i - n_chunks]
                p = jnp.exp2(s_scr[hh % 2, c * kc:(c + 1) * kc, :] - m_head[hh])
                o_c, l_c = _dot(v_blk(hh, c), p.astype(BF16)), jnp.sum(p, axis=0, keepdims=True)
                acc[hh] = o_c if acc[hh] is None else acc[hh] + o_c
                den[hh] = l_c if den[hh] is None else den[hh] + l_c
        finish(acc, den)


def _score_shifts(kn2, qn2, lb, tile):
    B, _, S = lb.shape
    kmax2 = jnp.max(kn2[:, :, :MLA_HEADS], axis=1)
    ub = jnp.sqrt(qn2 * kmax2[:, :, None]) * ATTN_NORM_SLACK
    ok = (ub - lb) <= ATTN_SAFE_GAP
    safe = jnp.all(ok.reshape(B, MLA_HEADS, S // tile, tile), axis=(1, 3))
    return lb + ATTN_SHIFT_MARGIN, safe.astype(jnp.int32)


def _attn(q_t, k, v_t, shift, safe, tile):
    B, _, S, _ = k.shape
    hp = MLA_HEADS * HEAD_PAD
    return pl.pallas_call(
        _attn_kernel,
        grid_spec=pltpu.PrefetchScalarGridSpec(
            num_scalar_prefetch=1,
            grid=(B, S // tile),
            in_specs=[pl.BlockSpec((1, hp, tile), lambda b, t, safe: (b, 0, t)),
                      pl.BlockSpec((1, MLA_HEADS, S, HEAD_PAD), lambda b, t, safe: (b, 0, 0, 0)),
                      pl.BlockSpec((1, MLA_WIDTH, S), lambda b, t, safe: (b, 0, 0)),
                      pl.BlockSpec((1, MLA_HEADS, tile), lambda b, t, safe: (b, 0, t))],
            out_specs=pl.BlockSpec((1, tile, MLA_WIDTH), lambda b, t, safe: (b, t, 0)),
            scratch_shapes=[pltpu.VMEM((2, S, tile), F32)]),
        out_shape=jax.ShapeDtypeStruct((B, S, MLA_WIDTH), BF16),
        compiler_params=_params(2),
        name="attn",
    )(safe, q_t, k, v_t, shift)


def _bit_reverse(i, bits):
    return int(format(i, "0%db" % bits)[::-1], 2)


def _dft_kernel(vr_ref, vi_ref, twc_ref, tws_ref, cs_ref, w_f_ref, g_ref, xr_scr, xi_scr, xb_scr, g_scr):
    S = vr_ref.shape[1]
    n_leaves = 2 ** DFT_LEVELS
    leaf = S // n_leaves
    chunk = min(DFT_ROW_CHUNK, leaf)
    reps = FNET_WIDTH // LANES
    n, off = S, 0
    for lev in range(DFT_LEVELS):
        half = n // 2
        last = lev == DFT_LEVELS - 1
        for base in range(0, S, n):
            for c0 in range(0, half, chunk):
                top = slice(base + c0, base + c0 + chunk)
                bot = slice(base + half + c0, base + half + c0 + chunk)
                if lev == 0:
                    tr, ti = vr_ref[0, top, :].astype(F32), vi_ref[0, top, :].astype(F32)
                    br, bi = vr_ref[0, bot, :].astype(F32), vi_ref[0, bot, :].astype(F32)
                else:
                    tr, ti, br, bi = xr_scr[top, :], xi_scr[top, :], xr_scr[bot, :], xi_scr[bot, :]
                c = jnp.concatenate([twc_ref[off + c0:off + c0 + chunk, :]] * reps, axis=1)
                s = jnp.concatenate([tws_ref[off + c0:off + c0 + chunk, :]] * reps, axis=1)
                dr, di = tr - br, ti - bi
                ar, ai = tr + br, ti + bi
                mr, mi = dr * c + di * s, di * c - dr * s
                if last:
                    blk, r0 = (base // n) * 2, c0
                    xb_scr[blk, r0:r0 + chunk, :] = ar.astype(BF16)
                    xb_scr[blk, leaf + r0:leaf + r0 + chunk, :] = ai.astype(BF16)
                    xb_scr[blk + 1, r0:r0 + chunk, :] = mr.astype(BF16)
                    xb_scr[blk + 1, leaf + r0:leaf + r0 + chunk, :] = mi.astype(BF16)
                else:
                    xr_scr[top, :], xi_scr[top, :] = ar, ai
                    xr_scr[bot, :], xi_scr[bot, :] = mr, mi
        off += half
        n = half
    for blk in range(n_leaves):
        f = _dot(cs_ref[...], xb_scr[blk])
        g = _dot(f.astype(BF16), w_f_ref[...])
        r = _bit_reverse(blk, DFT_LEVELS)
        for lt in range(reps):
            g_scr[lt, pl.ds(r, leaf, stride=n_leaves), :] = g[:, lt * LANES:(lt + 1) * LANES]
    for lt in range(reps):
        g_ref[0, :, lt * LANES:(lt + 1) * LANES] = g_scr[lt].astype(BF16)


def _dft(vr, vi, twc, tws, cs, w_f):
    B, S, _ = vr.shape
    n_leaves = 2 ** DFT_LEVELS
    leaf = S // n_leaves
    bat = pl.BlockSpec((1, S, FNET_WIDTH), lambda b: (b, 0, 0))
    return pl.pallas_call(
        _dft_kernel,
        grid=(B,),
        in_specs=[bat, bat, _const_spec(twc.shape), _const_spec(tws.shape), _const_spec(cs.shape),
                  _const_spec(w_f.shape)],
        out_specs=bat,
        out_shape=jax.ShapeDtypeStruct((B, S, FNET_WIDTH), BF16),
        scratch_shapes=[pltpu.VMEM((S, FNET_WIDTH), F32), pltpu.VMEM((S, FNET_WIDTH), F32),
                        pltpu.VMEM((n_leaves, 2 * leaf, FNET_WIDTH), BF16),
                        pltpu.VMEM((FNET_WIDTH // LANES, S, LANES), F32)],
        compiler_params=_params(1),
        name="dft",
    )(vr, vi, twc, tws, cs, w_f)


def _post_kernel(x_ref, o_ref, g_ref, mem_ref, w_out_ref, ln_x_g_ref, mem_g_ref, w_xq_ref, w_xkv_ref,
                 w_xo_ref, ln_f_g_ref, w_gu_ref, w_d_ref, fin_g_ref, y_ref, kv_scr, a_scr):
    @pl.when(pl.program_id(1) == 0)
    def _():
        m = _rms(mem_ref[0], mem_g_ref[...]).astype(BF16)
        kv_scr[...] = _dot(m, w_xkv_ref[...]).astype(BF16)

    x1 = (x_ref[0] + _dot(o_ref[0], w_out_ref[:MLA_WIDTH, :])
          + _dot(g_ref[0], w_out_ref[MLA_WIDTH:, :]))

    hq = _rms(x1, ln_x_g_ref[...]).astype(BF16)
    q = (_dot(hq, w_xq_ref[...]) * (XATTN_HEAD_DIM ** -0.5)).astype(BF16)
    heads = []
    for hd in range(XATTN_HEADS):
        sl = slice(hd * XATTN_HEAD_DIM, (hd + 1) * XATTN_HEAD_DIM)
        s = _dot_nt(q[:, sl], kv_scr[:, sl])
        m = jnp.max(s, axis=-1, keepdims=True)
        p = jnp.exp(s - m)
        l = jnp.sum(p, axis=-1, keepdims=True)
        vh = kv_scr[:, D_MODEL + hd * XATTN_HEAD_DIM:D_MODEL + (hd + 1) * XATTN_HEAD_DIM]
        heads.append((_dot(p.astype(BF16), vh) / l).astype(BF16))
    x2 = x1 + _dot(jnp.concatenate(heads, axis=-1), w_xo_ref[...])

    hf = _rms(x2, ln_f_g_ref[...]).astype(BF16)
    for c in range(D_FF // FF_CHUNK):
        sl = slice(c * FF_CHUNK, (c + 1) * FF_CHUNK)
        gate = _dot(hf, w_gu_ref[:, sl])
        up = _dot(hf, w_gu_ref[:, D_FF + c * FF_CHUNK:D_FF + (c + 1) * FF_CHUNK])
        a_scr[:, sl] = (gate * jax.nn.sigmoid(gate) * up).astype(BF16)
    x3 = x2 + _dot(a_scr[...], w_d_ref[...])
    y_ref[0] = _rms(x3, fin_g_ref[...])


def _post(x, o, g, mem, w_out, ln_x_g, mem_g, w_xq, w_xkv, w_xo, ln_f_g, w_gu, w_d, fin_g, tile):
    B, S, _ = x.shape
    M = mem.shape[1]
    tok = lambda w: pl.BlockSpec((1, tile, w), lambda b, t: (b, t, 0))
    consts = [w_out, ln_x_g, mem_g, w_xq, w_xkv, w_xo, ln_f_g, w_gu, w_d, fin_g]
    return pl.pallas_call(
        _post_kernel,
        grid=(B, S // tile),
        in_specs=[tok(D_MODEL), tok(MLA_WIDTH), tok(FNET_WIDTH),
                  pl.BlockSpec((1, M, D_MODEL), lambda b, t: (b, 0, 0))]
                 + [_const_spec(c.shape) for c in consts],
        out_specs=tok(D_MODEL),
        out_shape=jax.ShapeDtypeStruct((B, S, D_MODEL), F32),
        scratch_shapes=[pltpu.VMEM((M, 2 * D_MODEL), BF16), pltpu.VMEM((tile, D_FF), BF16)],
        compiler_params=_params(2),
        name="post",
    )(x, o, g, mem, *consts)


def _rot_cols(w):
    half = QK_ROPE_DIM // 2
    return jnp.concatenate([-w[..., half:], w[..., :half]], axis=-1)


def _prep_weights(w_in, w_uq, w_ukv, w_fnet):
    w_cq = w_in[:, :Q_LORA_RANK]
    w_ckv = w_in[:, Q_LORA_RANK:Q_LORA_RANK + KV_LORA_RANK]
    o3 = Q_LORA_RANK + KV_LORA_RANK
    w_kr = w_in[:, o3:o3 + QK_ROPE_DIM]
    w_u = w_in[:, o3 + QK_ROPE_DIM:]
    w_kr_grp = jnp.concatenate([jnp.zeros((D_MODEL, QK_NOPE_DIM), F32), w_kr, _rot_cols(w_kr)], axis=-1)
    w_in_ext = jnp.concatenate([w_cq, w_ckv, w_kr_grp, w_u], axis=-1).astype(BF16)

    w_q_t = w_uq.T.astype(BF16)

    kv_w = w_ukv.reshape(KV_LORA_RANK, MLA_HEADS, QK_NOPE_DIM + V_HEAD_DIM)
    w_uv_t = kv_w[..., QK_NOPE_DIM:].reshape(KV_LORA_RANK, MLA_WIDTH).T.astype(BF16)

    w_f = jnp.zeros((FNET_WIDTH, FNET_WIDTH), F32)
    for gi in range(FNET_GROUPS):
        sl = slice(gi * FNET_GROUP_DIM, (gi + 1) * FNET_GROUP_DIM)
        w_f = w_f.at[sl, sl].set(w_fnet[gi])
    return w_in_ext, w_q_t, w_uv_t, w_f.astype(BF16)


def _dft_mats(n):
    idx = np.arange(n, dtype=np.int64)
    ang = ((idx[:, None] * idx[None, :]) % n) * (2.0 * np.pi / n)
    return np.cos(ang), np.sin(ang)


def _seq_dft_tables(seq):
    cs_rows, sn_rows = [], []
    n = seq
    for _ in range(DFT_LEVELS):
        ang = np.arange(n // 2) * (2.0 * np.pi / n)
        cs_rows.append(np.cos(ang))
        sn_rows.append(np.sin(ang))
        n //= 2
    rep = lambda rows: jnp.asarray(np.broadcast_to(np.concatenate(rows)[:, None], (seq - n, LANES)), F32)
    c, s = _dft_mats(n)
    leaf = jnp.asarray(np.concatenate([c, s], axis=1) * seq ** -0.5, F32).astype(BF16)
    return rep(cs_rows), rep(sn_rows), leaf


def _channel_dft():
    c, s = _dft_mats(FNET_GROUP_DIM)
    return jnp.asarray(np.concatenate([c, -s], axis=-1) * FNET_GROUP_DIM ** -0.5, F32).astype(BF16)


def _rope_tabs(seq):
    inv = 1.0 / (ROPE_BASE ** (np.arange(0, QK_ROPE_DIM, 2) / QK_ROPE_DIM))
    ang = np.arange(seq)[:, None] * inv[None, :]
    cos, sin = np.cos(ang), np.sin(ang)
    pad = lambda t: jnp.asarray(np.concatenate(
        [np.zeros((seq, QK_NOPE_DIM)), t, t, np.zeros((seq, HEAD_PAD - QK_NOPE_DIM - QK_ROPE_DIM))], axis=-1), F32)
    return pad(cos), pad(sin), jnp.asarray(cos.T, F32), jnp.asarray(sin.T, F32)


def _trunk(x, mem, w, tiles):
    S = x.shape[1]
    ctab, stab, cos_t, sin_t = _rope_tabs(S)
    twc, tws, cs = _seq_dft_tables(S)
    q_t, k, v_t, kn2, qn2, lb, vr, vi = _proj(x, ctab, stab, cos_t, sin_t, w["ln_mix_g"], w["w_in_ext"],
                                              w["q_norm_g"], w["w_q_t"], w["kv_norm_g"], w["w_ukv"],
                                              w["w_uv_t"], w["w_cdft"], min(tiles[0], S))
    shift, safe = _score_shifts(kn2, qn2, lb, min(tiles[1], S))
    o = _attn(q_t, k, v_t, shift, safe, min(tiles[1], S))
    g = _dft(vr, vi, twc, tws, cs, w["w_f"])
    return _post(x, o, g, mem, w["w_out"], w["ln_x_g"], w["mem_norm_g"], w["w_xq"], w["w_xkv"], w["w_xo"],
                 w["ln_ffn_g"], w["w_gate_up"], w["w_down"], w["final_norm_g"], min(tiles[2], S))


def kernel(x_prompt, x_sample, mem_prompt, mem_sample, ln_mix_g, w_in, q_norm_g, w_uq, kv_norm_g, w_ukv, w_fnet,
           w_out, ln_x_g, mem_norm_g, w_xq, w_xkv, w_xo, ln_ffn_g, w_gate_up, w_down, final_norm_g):
    assert ln_mix_g.shape[0] == 1, "single-layer trunk"
    w_in_ext, w_q_t, w_uv_t, w_f = _prep_weights(w_in[0], w_uq[0], w_ukv[0], w_fnet[0])
    row = lambda g: g.reshape(1, -1).astype(F32)
    w = dict(
        ln_mix_g=row(ln_mix_g[0]), w_in_ext=w_in_ext, q_norm_g=row(q_norm_g[0]), w_q_t=w_q_t,
        kv_norm_g=row(kv_norm_g[0]), w_ukv=w_ukv[0].astype(BF16), w_uv_t=w_uv_t, w_cdft=_channel_dft(), w_f=w_f,
        w_out=w_out[0].astype(BF16), ln_x_g=row(ln_x_g[0]), mem_norm_g=row(mem_norm_g[0]),
        w_xq=w_xq[0].astype(BF16), w_xkv=w_xkv[0].astype(BF16), w_xo=w_xo[0].astype(BF16),
        ln_ffn_g=row(ln_ffn_g[0]), w_gate_up=w_gate_up[0].astype(BF16), w_down=w_down[0].astype(BF16),
        final_norm_g=row(final_norm_g),
    )
    return (_trunk(x_prompt, mem_prompt, w, TILES), _trunk(x_sample, mem_sample, w, TILES))
```

```python
import functools

import numpy as np
import jax
import jax.numpy as jnp
from jax import lax
from jax.experimental import pallas as pl
from jax.experimental.pallas import tpu as pltpu

F32 = jnp.float32
BF16 = jnp.bfloat16

D_MODEL = 1024
MLA_HEADS = 8
QK_NOPE_DIM = 64
QK_ROPE_DIM = 32
V_HEAD_DIM = 64
Q_LORA_RANK = 384
KV_LORA_RANK = 256
FNET_GROUPS = 4
FNET_GROUP_DIM = 128
FNET_WIDTH = FNET_GROUPS * FNET_GROUP_DIM
MLA_WIDTH = MLA_HEADS * V_HEAD_DIM
XATTN_HEADS = 4
XATTN_HEAD_DIM = D_MODEL // XATTN_HEADS
D_FF = 2816
ROPE_BASE = 10000.0
NORM_EPS = 1e-6
LOG2_E = 1.4426950408889634

LANES = 128
HEAD_PAD = LANES
FF_CHUNK = 256
VMEM_LIMIT = 56 * 1024 * 1024
TILES = (1024, 512, 512)
ATTN_KEY_CHUNK = 1024
ATTN_PILOT_KEYS = 64
ATTN_SHIFT_MARGIN = 30.0
ATTN_SAFE_GAP = 90.0
ATTN_NORM_SLACK = 1.02
DFT_LEVELS = 3
DFT_ROW_CHUNK = 256

_O_CQ = 0
_O_CKV = _O_CQ + Q_LORA_RANK
_O_KR = _O_CKV + KV_LORA_RANK
_O_U = _O_KR + HEAD_PAD
IN_EXT = _O_U + FNET_WIDTH


def _rms(x, g):
    return x * lax.rsqrt(jnp.mean(x * x, axis=-1, keepdims=True) + NORM_EPS) * g


def _dot(a, b):
    return jnp.dot(a, b, preferred_element_type=F32)


def _dot_nt(a, b):
    return lax.dot_general(a, b, (((1,), (1,)), ((), ())), preferred_element_type=F32)


def _const_spec(shape):
    zeros = (0,) * len(shape)
    return pl.BlockSpec(shape, lambda *_: zeros, pipeline_mode=pl.Buffered(1))


def _params(n_axes, flags=None):
    return pltpu.CompilerParams(dimension_semantics=("arbitrary",) * n_axes,
                                vmem_limit_bytes=VMEM_LIMIT, flags=flags)


def _proj_kernel(x_ref, ctab_ref, stab_ref, cos_t_ref, sin_t_ref, ln_g_ref, w_in_ref, qn_g_ref, w_q_t_ref,
                 kvn_g_ref, w_ukv_ref, w_uv_t_ref, w_cdft_ref, head_sel_ref,
                 q_t_ref, k_ref, v_t_ref, kn2_ref, qn2_ref, lb_ref, vr_ref, vi_ref):
    scale = (QK_NOPE_DIM + QK_ROPE_DIM) ** -0.5 * LOG2_E
    half = QK_ROPE_DIM // 2
    qk_dim = QK_NOPE_DIM + QK_ROPE_DIM
    h = _rms(x_ref[0], ln_g_ref[...]).astype(BF16)
    z = _dot(h, w_in_ref[...])

    cq = _rms(z[:, _O_CQ:_O_CQ + Q_LORA_RANK], qn_g_ref[...]).astype(BF16)
    q_t = _dot_nt(w_q_t_ref[...], cq)
    cos_t = cos_t_ref[...] * scale
    sin_t = sin_t_ref[...] * scale
    zero_rows = jnp.zeros((HEAD_PAD - qk_dim, q_t.shape[1]), BF16)
    q_heads = []
    for hd in range(MLA_HEADS):
        src, dst = hd * qk_dim, hd * HEAD_PAD
        x1 = q_t[src + QK_NOPE_DIM:src + QK_NOPE_DIM + half, :]
        x2 = q_t[src + QK_NOPE_DIM + half:src + qk_dim, :]
        q_h = jnp.concatenate([q_t[src:src + QK_NOPE_DIM, :] * scale, x1 * cos_t - x2 * sin_t,
                               x2 * cos_t + x1 * sin_t], axis=0)
        q_heads.append(jnp.concatenate([q_h.astype(BF16), zero_rows], axis=0))
        q_t_ref[0, dst:dst + HEAD_PAD, :] = q_heads[hd]
        qn2_ref[0, hd:hd + 1, :] = jnp.sum(q_h * q_h, axis=0, keepdims=True)

    ckv = _rms(z[:, _O_CKV:_O_CKV + KV_LORA_RANK], kvn_g_ref[...]).astype(BF16)
    kv = _dot(ckv, w_ukv_ref[...])
    zk = z[:, _O_KR:_O_KR + HEAD_PAD]
    kr = zk * ctab_ref[...] + pltpu.roll(zk, HEAD_PAD - QK_ROPE_DIM, 1) * stab_ref[...]
    nope = lax.broadcasted_iota(jnp.int32, kr.shape, 1) < QK_NOPE_DIM
    k_heads = [jnp.where(nope, kv[:, hd * HEAD_PAD:(hd + 1) * HEAD_PAD], kr) for hd in range(MLA_HEADS)]
    for hd in range(MLA_HEADS):
        k_ref[0, hd] = k_heads[hd].astype(BF16)
    k_all = jnp.concatenate(k_heads, axis=1)
    kn2_ref[0] = _dot((k_all * k_all).astype(BF16), head_sel_ref[...])

    pilot_k = k_all[:ATTN_PILOT_KEYS, :].astype(BF16)
    for hd in range(MLA_HEADS):
        pilot = _dot(pilot_k[:, hd * HEAD_PAD:(hd + 1) * HEAD_PAD], q_heads[hd])
        lb_ref[0, hd:hd + 1, :] = jnp.max(pilot, axis=0, keepdims=True)
    v_t_ref[0] = _dot_nt(w_uv_t_ref[...], ckv).astype(BF16)

    for gi in range(FNET_GROUPS):
        sl = slice(gi * FNET_GROUP_DIM, (gi + 1) * FNET_GROUP_DIM)
        v = _dot(z[:, _O_U + gi * FNET_GROUP_DIM:_O_U + (gi + 1) * FNET_GROUP_DIM].astype(BF16), w_cdft_ref[...])
        vr_ref[0, :, sl] = v[:, :FNET_GROUP_DIM].astype(BF16)
        vi_ref[0, :, sl] = v[:, FNET_GROUP_DIM:].astype(BF16)


def _proj(x, ctab, stab, cos_t, sin_t, ln_g, w_in_ext, qn_g, w_q_t, kvn_g, w_ukv, w_uv_t, w_cdft, tile):
    B, S, _ = x.shape
    hp = MLA_HEADS * HEAD_PAD
    tok = lambda w: pl.BlockSpec((1, tile, w), lambda b, t: (b, t, 0))
    tok_t = lambda w: pl.BlockSpec((1, w, tile), lambda b, t: (b, 0, t))
    tab = pl.BlockSpec((tile, LANES), lambda b, t: (t, 0))
    tab_t = pl.BlockSpec((cos_t.shape[0], tile), lambda b, t: (0, t))
    head_sel = jnp.asarray(np.arange(hp)[:, None] // HEAD_PAD == np.arange(LANES)[None, :], BF16)
    return pl.pallas_call(
        _proj_kernel,
        grid=(B, S // tile),
        in_specs=[tok(D_MODEL), tab, tab, tab_t, tab_t, _const_spec(ln_g.shape), _const_spec(w_in_ext.shape),
                  _const_spec(qn_g.shape), _const_spec(w_q_t.shape), _const_spec(kvn_g.shape),
                  _const_spec(w_ukv.shape), _const_spec(w_uv_t.shape), _const_spec(w_cdft.shape),
                  _const_spec(head_sel.shape)],
        out_specs=[tok_t(hp), pl.BlockSpec((1, MLA_HEADS, tile, HEAD_PAD), lambda b, t: (b, 0, t, 0)),
                   tok_t(MLA_WIDTH), tok(LANES), tok_t(MLA_HEADS), tok_t(MLA_HEADS),
                   tok(FNET_WIDTH), tok(FNET_WIDTH)],
        out_shape=[jax.ShapeDtypeStruct((B, hp, S), BF16), jax.ShapeDtypeStruct((B, MLA_HEADS, S, HEAD_PAD), BF16),
                   jax.ShapeDtypeStruct((B, MLA_WIDTH, S), BF16), jax.ShapeDtypeStruct((B, S, LANES), F32),
                   jax.ShapeDtypeStruct((B, MLA_HEADS, S), F32), jax.ShapeDtypeStruct((B, MLA_HEADS, S), F32)]
                  + [jax.ShapeDtypeStruct((B, S, FNET_WIDTH), BF16)] * 2,
        compiler_params=_params(2),
        name="proj",
    )(x, ctab, stab, cos_t, sin_t, ln_g, w_in_ext, qn_g, w_q_t, kvn_g, w_ukv, w_uv_t, w_cdft, head_sel)


def _attn_kernel(safe_ref, q_t_ref, k_ref, v_t_ref, shift_ref, o_ref, s_scr):
    S = k_ref.shape[2]
    kc = min(ATTN_KEY_CHUNK, S)
    n_chunks = S // kc
    heads = range(MLA_HEADS)

    def k_blk(hh, r0, r1):
        return k_ref[0, hh, r0:r1, :]

    def q_blk(hh):
        return q_t_ref[0, hh * HEAD_PAD:(hh + 1) * HEAD_PAD, :]

    def v_blk(hh, c):
        return v_t_ref[0, hh * V_HEAD_DIM:(hh + 1) * V_HEAD_DIM, c * kc:(c + 1) * kc]

    def finish(acc, den):
        o_ref[0] = jnp.concatenate([a / l for a, l in zip(acc, den)], axis=0).T.astype(BF16)

    all_safe = safe_ref[pl.program_id(0), pl.program_id(1)] != 0

    @pl.when(all_safe)
    def _single_pass():
        items = [(hh, c) for hh in heads for c in range(n_chunks)]
        score = lambda hh, c: _dot(k_blk(hh, c * kc, (c + 1) * kc), q_blk(hh))
        acc, den = [None] * len(heads), [None] * len(heads)
        s_next = score(*items[0])
        for i, (hh, c) in enumerate(items):
            s = s_next
            if i + 1 < len(items):
                s_next = score(*items[i + 1])
            p = jnp.exp2(s - shift_ref[0, hh:hh + 1, :])
            o_c, l_c = _dot(v_blk(hh, c), p.astype(BF16)), jnp.sum(p, axis=0, keepdims=True)
            acc[hh] = o_c if acc[hh] is None else acc[hh] + o_c
            den[hh] = l_c if den[hh] is None else den[hh] + l_c
        finish(acc, den)

    @pl.when(jnp.logical_not(all_safe))
    def _two_pass():
        def score_chunk(hh, c):
            s = _dot(k_blk(hh, c * kc, (c + 1) * kc), q_blk(hh))
            s_scr[hh % 2, c * kc:(c + 1) * kc, :] = s
            return jnp.max(s, axis=0, keepdims=True)

        items = [(hh, c) for hh in heads for c in range(n_chunks)]
        m_parts = [[] for _ in heads]
        m_head, acc, den = [None] * len(heads), [None] * len(heads), [None] * len(heads)
        for i in range(len(items) + n_chunks):
            if i < len(items):
                hh, c = items[i]
                m_parts[hh].append(score_chunk(hh, c))
                if c == n_chunks - 1:
                    m_head[hh] = functools.reduce(jnp.maximum, m_parts[hh])
            if i >= n_chunks:
                hh, c = items[i - n_chunks]
                p = jnp.exp2(s_scr[hh % 2, c * kc:(c + 1) * kc, :] - m_head[hh])
                o_c, l_c = _dot(v_blk(hh, c), p.astype(BF16)), jnp.sum(p, axis=0, keepdims=True)
                acc[hh] = o_c if acc[hh] is None else acc[hh] + o_c
                den[hh] = l_c if den[hh] is None else den[hh] + l_c
        finish(acc, den)


def _score_shifts(kn2, qn2, lb, tile):
    B, _, S = lb.shape
    kmax2 = jnp.max(kn2[:, :, :MLA_HEADS], axis=1)
    ub = jnp.sqrt(qn2 * kmax2[:, :, None]) * ATTN_NORM_SLACK
    ok = (ub - lb) <= ATTN_SAFE_GAP
    safe = jnp.all(ok.reshape(B, MLA_HEADS, S // tile, tile), axis=(1, 3))
    return lb + ATTN_SHIFT_MARGIN, safe.astype(jnp.int32)


def _attn(q_t, k, v_t, shift, safe, tile):
    B, _, S, _ = k.shape
    hp = MLA_HEADS * HEAD_PAD
    return pl.pallas_call(
        _attn_kernel,
        grid_spec=pltpu.PrefetchScalarGridSpec(
            num_scalar_prefetch=1,
            grid=(B, S // tile),
            in_specs=[pl.BlockSpec((1, hp, tile), lambda b, t, safe: (b, 0, t)),
                      pl.BlockSpec((1, MLA_HEADS, S, HEAD_PAD), lambda b, t, safe: (b, 0, 0, 0)),
                      pl.BlockSpec((1, MLA_WIDTH, S), lambda b, t, safe: (b, 0, 0)),
                      pl.BlockSpec((1, MLA_HEADS, tile), lambda b, t, safe: (b, 0, t))],
            out_specs=pl.BlockSpec((1, tile, MLA_WIDTH), lambda b, t, safe: (b, t, 0)),
            scratch_shapes=[pltpu.VMEM((2, S, tile), F32)]),
        out_shape=jax.ShapeDtypeStruct((B, S, MLA_WIDTH), BF16),
        compiler_params=_params(2),
        name="attn",
    )(safe, q_t, k, v_t, shift)


def _bit_reverse(i, bits):
    return int(format(i, "0%db" % bits)[::-1], 2)


def _dft_kernel(vr_ref, vi_ref, twc_ref, tws_ref, cs_ref, w_f_ref, g_ref, xr_scr, xi_scr, xb_scr):
    S = vr_ref.shape[1]
    n_leaves = 2 ** DFT_LEVELS
    leaf = S // n_leaves
    chunk = min(DFT_ROW_CHUNK, leaf)
    reps = FNET_WIDTH // LANES
    n, off = S, 0
    for lev in range(DFT_LEVELS):
        half = n // 2
        last = lev == DFT_LEVELS - 1
        for base in range(0, S, n):
            for c0 in range(0, half, chunk):
                top = slice(base + c0, base + c0 + chunk)
                bot = slice(base + half + c0, base + half + c0 + chunk)
                if lev == 0:
                    tr, ti = vr_ref[0, top, :].astype(F32), vi_ref[0, top, :].astype(F32)
                    br, bi = vr_ref[0, bot, :].astype(F32), vi_ref[0, bot, :].astype(F32)
                else:
                    tr, ti, br, bi = xr_scr[top, :], xi_scr[top, :], xr_scr[bot, :], xi_scr[bot, :]
                c = jnp.concatenate([twc_ref[off + c0:off + c0 + chunk, :]] * reps, axis=1)
                s = jnp.concatenate([tws_ref[off + c0:off + c0 + chunk, :]] * reps, axis=1)
                dr, di = tr - br, ti - bi
                ar, ai = tr + br, ti + bi
                mr, mi = dr * c + di * s, di * c - dr * s
                if last:
                    blk, r0 = (base // n) * 2, c0
                    xb_scr[blk, r0:r0 + chunk, :] = ar.astype(BF16)
                    xb_scr[blk, leaf + r0:leaf + r0 + chunk, :] = ai.astype(BF16)
                    xb_scr[blk + 1, r0:r0 + chunk, :] = mr.astype(BF16)
                    xb_scr[blk + 1, leaf + r0:leaf + r0 + chunk, :] = mi.astype(BF16)
                else:
                    xr_scr[top, :], xi_scr[top, :] = ar, ai
                    xr_scr[bot, :], xi_scr[bot, :] = mr, mi
        off += half
        n = half
    for blk in range(n_leaves):
        f = _dot(cs_ref[...], xb_scr[blk])
        g = _dot(f.astype(BF16), w_f_ref[...])
        g_ref[0, _bit_reverse(blk, DFT_LEVELS)] = g.astype(BF16)


def _dft(vr, vi, twc, tws, cs, w_f):
    B, S, _ = vr.shape
    n_leaves = 2 ** DFT_LEVELS
    leaf = S // n_leaves
    bat = pl.BlockSpec((1, S, FNET_WIDTH), lambda b: (b, 0, 0))
    return pl.pallas_call(
        _dft_kernel,
        grid=(B,),
        in_specs=[bat, bat, _const_spec(twc.shape), _const_spec(tws.shape), _const_spec(cs.shape),
                  _const_spec(w_f.shape)],
        out_specs=pl.BlockSpec((1, n_leaves, leaf, FNET_WIDTH), lambda b: (b, 0, 0, 0)),
        out_shape=jax.ShapeDtypeStruct((B, n_leaves, leaf, FNET_WIDTH), BF16),
        scratch_shapes=[pltpu.VMEM((S, FNET_WIDTH), F32), pltpu.VMEM((S, FNET_WIDTH), F32),
                        pltpu.VMEM((n_leaves, 2 * leaf, FNET_WIDTH), BF16)],
        compiler_params=_params(1),
        name="dft",
    )(vr, vi, twc, tws, cs, w_f)


def _post_kernel(x_ref, o_ref, g_ref, mem_ref, w_out_ref, ln_x_g_ref, mem_g_ref, w_xq_ref, w_xkv_ref,
                 w_xo_ref, ln_f_g_ref, w_gu_ref, w_d_ref, fin_g_ref, y_ref, kv_scr, a_scr, g_scr):
    @pl.when(pl.program_id(1) == 0)
    def _():
        m = _rms(mem_ref[0], mem_g_ref[...]).astype(BF16)
        kv_scr[...] = _dot(m, w_xkv_ref[...]).astype(BF16)

    n_res, rows_per_res = g_ref.shape[1], g_ref.shape[2]
    for r in range(n_res):
        g_r = g_ref[0, r].astype(F32)
        for lt in range(FNET_WIDTH // LANES):
            g_scr[lt, pl.ds(r, rows_per_res, stride=n_res), :] = g_r[:, lt * LANES:(lt + 1) * LANES]
    g = jnp.concatenate([g_scr[lt] for lt in range(FNET_WIDTH // LANES)], axis=1).astype(BF16)

    x1 = (x_ref[0] + _dot(o_ref[0], w_out_ref[:MLA_WIDTH, :])
          + _dot(g, w_out_ref[MLA_WIDTH:, :]))

    hq = _rms(x1, ln_x_g_ref[...]).astype(BF16)
    q = (_dot(hq, w_xq_ref[...]) * (XATTN_HEAD_DIM ** -0.5)).astype(BF16)
    heads = []
    for hd in range(XATTN_HEADS):
        sl = slice(hd * XATTN_HEAD_DIM, (hd + 1) * XATTN_HEAD_DIM)
        s = _dot_nt(q[:, sl], kv_scr[:, sl])
        m = jnp.max(s, axis=-1, keepdims=True)
        p = jnp.exp(s - m)
        l = jnp.sum(p, axis=-1, keepdims=True)
        vh = kv_scr[:, D_MODEL + hd * XATTN_HEAD_DIM:D_MODEL + (hd + 1) * XATTN_HEAD_DIM]
        heads.append((_dot(p.astype(BF16), vh) / l).astype(BF16))
    x2 = x1 + _dot(jnp.concatenate(heads, axis=-1), w_xo_ref[...])

    hf = _rms(x2, ln_f_g_ref[...]).astype(BF16)
    for c in range(D_FF // FF_CHUNK):
        sl = slice(c * FF_CHUNK, (c + 1) * FF_CHUNK)
        gate = _dot(hf, w_gu_ref[:, sl])
        up = _dot(hf, w_gu_ref[:, D_FF + c * FF_CHUNK:D_FF + (c + 1) * FF_CHUNK])
        a_scr[:, sl] = (gate * jax.nn.sigmoid(gate) * up).astype(BF16)
    x3 = x2 + _dot(a_scr[...], w_d_ref[...])
    y_ref[0] = _rms(x3, fin_g_ref[...])


def _post(x, o, g, mem, w_out, ln_x_g, mem_g, w_xq, w_xkv, w_xo, ln_f_g, w_gu, w_d, fin_g, tile):
    B, S, _ = x.shape
    M = mem.shape[1]
    n_res = g.shape[1]
    tok = lambda w: pl.BlockSpec((1, tile, w), lambda b, t: (b, t, 0))
    consts = [w_out, ln_x_g, mem_g, w_xq, w_xkv, w_xo, ln_f_g, w_gu, w_d, fin_g]
    return pl.pallas_call(
        _post_kernel,
        grid=(B, S // tile),
        in_specs=[tok(D_MODEL), tok(MLA_WIDTH),
                  pl.BlockSpec((1, n_res, tile // n_res, FNET_WIDTH), lambda b, t: (b, 0, t, 0)),
                  pl.BlockSpec((1, M, D_MODEL), lambda b, t: (b, 0, 0))]
                 + [_const_spec(c.shape) for c in consts],
        out_specs=tok(D_MODEL),
        out_shape=jax.ShapeDtypeStruct((B, S, D_MODEL), F32),
        scratch_shapes=[pltpu.VMEM((M, 2 * D_MODEL), BF16), pltpu.VMEM((tile, D_FF), BF16),
                        pltpu.VMEM((FNET_WIDTH // LANES, tile, LANES), F32)],
        compiler_params=_params(2),
        name="post",
    )(x, o, g, mem, *consts)


def _rot_cols(w):
    half = QK_ROPE_DIM // 2
    return jnp.concatenate([-w[..., half:], w[..., :half]], axis=-1)


def _prep_weights(w_in, w_uq, w_ukv, w_fnet):
    w_cq = w_in[:, :Q_LORA_RANK]
    w_ckv = w_in[:, Q_LORA_RANK:Q_LORA_RANK + KV_LORA_RANK]
    o3 = Q_LORA_RANK + KV_LORA_RANK
    w_kr = w_in[:, o3:o3 + QK_ROPE_DIM]
    w_u = w_in[:, o3 + QK_ROPE_DIM:]
    w_kr_grp = jnp.concatenate([jnp.zeros((D_MODEL, QK_NOPE_DIM), F32), w_kr, _rot_cols(w_kr)], axis=-1)
    w_in_ext = jnp.concatenate([w_cq, w_ckv, w_kr_grp, w_u], axis=-1).astype(BF16)

    w_q_t = w_uq.T.astype(BF16)

    kv_w = w_ukv.reshape(KV_LORA_RANK, MLA_HEADS, QK_NOPE_DIM + V_HEAD_DIM)
    w_uv_t = kv_w[..., QK_NOPE_DIM:].reshape(KV_LORA_RANK, MLA_WIDTH).T.astype(BF16)

    w_f = jnp.zeros((FNET_WIDTH, FNET_WIDTH), F32)
    for gi in range(FNET_GROUPS):
        sl = slice(gi * FNET_GROUP_DIM, (gi + 1) * FNET_GROUP_DIM)
        w_f = w_f.at[sl, sl].set(w_fnet[gi])
    return w_in_ext, w_q_t, w_uv_t, w_f.astype(BF16)


def _dft_mats(n):
    idx = np.arange(n, dtype=np.int64)
    ang = ((idx[:, None] * idx[None, :]) % n) * (2.0 * np.pi / n)
    return np.cos(ang), np.sin(ang)


def _seq_dft_tables(seq):
    cs_rows, sn_rows = [], []
    n = seq
    for _ in range(DFT_LEVELS):
        ang = np.arange(n // 2) * (2.0 * np.pi / n)
        cs_rows.append(np.cos(ang))
        sn_rows.append(np.sin(ang))
        n //= 2
    rep = lambda rows: jnp.asarray(np.broadcast_to(np.concatenate(rows)[:, None], (seq - n, LANES)), F32)
    c, s = _dft_mats(n)
    leaf = jnp.asarray(np.concatenate([c, s], axis=1) * seq ** -0.5, F32).astype(BF16)
    return rep(cs_rows), rep(sn_rows), leaf


def _channel_dft():
    c, s = _dft_mats(FNET_GROUP_DIM)
    return jnp.asarray(np.concatenate([c, -s], axis=-1) * FNET_GROUP_DIM ** -0.5, F32).astype(BF16)


def _rope_tabs(seq):
    inv = 1.0 / (ROPE_BASE ** (np.arange(0, QK_ROPE_DIM, 2) / QK_ROPE_DIM))
    ang = np.arange(seq)[:, None] * inv[None, :]
    cos, sin = np.cos(ang), np.sin(ang)
    pad = lambda t: jnp.asarray(np.concatenate(
        [np.zeros((seq, QK_NOPE_DIM)), t, t, np.zeros((seq, HEAD_PAD - QK_NOPE_DIM - QK_ROPE_DIM))], axis=-1), F32)
    return pad(cos), pad(sin), jnp.asarray(cos.T, F32), jnp.asarray(sin.T, F32)


def _trunk(x, mem, w, tiles):
    S = x.shape[1]
    ctab, stab, cos_t, sin_t = _rope_tabs(S)
    twc, tws, cs = _seq_dft_tables(S)
    q_t, k, v_t, kn2, qn2, lb, vr, vi = _proj(x, ctab, stab, cos_t, sin_t, w["ln_mix_g"], w["w_in_ext"],
                                              w["q_norm_g"], w["w_q_t"], w["kv_norm_g"], w["w_ukv"],
                                              w["w_uv_t"], w["w_cdft"], min(tiles[0], S))
    shift, safe = _score_shifts(kn2, qn2, lb, min(tiles[1], S))
    o = _attn(q_t, k, v_t, shift, safe, min(tiles[1], S))
    g = _dft(vr, vi, twc, tws, cs, w["w_f"])
    return _post(x, o, g, mem, w["w_out"], w["ln_x_g"], w["mem_norm_g"], w["w_xq"], w["w_xkv"], w["w_xo"],
                 w["ln_ffn_g"], w["w_gate_up"], w["w_down"], w["final_norm_g"], min(tiles[2], S))


def kernel(x_prompt, x_sample, mem_prompt, mem_sample, ln_mix_g, w_in, q_norm_g, w_uq, kv_norm_g, w_ukv, w_fnet,
           w_out, ln_x_g, mem_norm_g, w_xq, w_xkv, w_xo, ln_ffn_g, w_gate_up, w_down, final_norm_g):
    assert ln_mix_g.shape[0] == 1, "single-layer trunk"
    w_in_ext, w_q_t, w_uv_t, w_f = _prep_weights(w_in[0], w_uq[0], w_ukv[0], w_fnet[0])
    row = lambda g: g.reshape(1, -1).astype(F32)
    w = dict(
        ln_mix_g=row(ln_mix_g[0]), w_in_ext=w_in_ext, q_norm_g=row(q_norm_g[0]), w_q_t=w_q_t,
        kv_norm_g=row(kv_norm_g[0]), w_ukv=w_ukv[0].astype(BF16), w_uv_t=w_uv_t, w_cdft=_channel_dft(), w_f=w_f,
        w_out=w_out[0].astype(BF16), ln_x_g=row(ln_x_g[0]), mem_norm_g=row(mem_norm_g[0]),
        w_xq=w_xq[0].astype(BF16), w_xkv=w_xkv[0].astype(BF16), w_xo=w_xo[0].astype(BF16),
        ln_ffn_g=row(ln_ffn_g[0]), w_gate_up=w_gate_up[0].astype(BF16), w_down=w_down[0].astype(BF16),
        final_norm_g=row(final_norm_g),
    )
    return (_trunk(x_prompt, mem_prompt, w, TILES), _trunk(x_sample, mem_sample, w, TILES))
```

```python
import functools

import numpy as np
import jax
import jax.numpy as jnp
from jax import lax
from jax.experimental import pallas as pl
from jax.experimental.pallas import tpu as pltpu

F32 = jnp.float32
BF16 = jnp.bfloat16

D_MODEL = 1024
MLA_HEADS = 8
QK_NOPE_DIM = 64
QK_ROPE_DIM = 32
V_HEAD_DIM = 64
Q_LORA_RANK = 384
KV_LORA_RANK = 256
FNET_GROUPS = 4
FNET_GROUP_DIM = 128
FNET_WIDTH = FNET_GROUPS * FNET_GROUP_DIM
MLA_WIDTH = MLA_HEADS * V_HEAD_DIM
XATTN_HEADS = 4
XATTN_HEAD_DIM = D_MODEL // XATTN_HEADS
D_FF = 2816
ROPE_BASE = 10000.0
NORM_EPS = 1e-6
LOG2_E = 1.4426950408889634

LANES = 128
HEAD_PAD = LANES
FF_CHUNK = 256
VMEM_LIMIT = 56 * 1024 * 1024
TILES = (1024, 1024, 512)
ATTN_KEY_CHUNK = 1024
ATTN_PILOT_KEYS = 64
ATTN_SHIFT_MARGIN = 30.0
ATTN_SAFE_GAP = 90.0
ATTN_NORM_SLACK = 1.02
DFT_LEVELS = 3
DFT_ROW_CHUNK = 256

_O_CQ = 0
_O_CKV = _O_CQ + Q_LORA_RANK
_O_KR = _O_CKV + KV_LORA_RANK
_O_U = _O_KR + HEAD_PAD
IN_EXT = _O_U + FNET_WIDTH


def _rms(x, g):
    return x * lax.rsqrt(jnp.mean(x * x, axis=-1, keepdims=True) + NORM_EPS) * g


def _dot(a, b):
    return jnp.dot(a, b, preferred_element_type=F32)


def _dot_nt(a, b):
    return lax.dot_general(a, b, (((1,), (1,)), ((), ())), preferred_element_type=F32)


def _const_spec(shape):
    zeros = (0,) * len(shape)
    return pl.BlockSpec(shape, lambda *_: zeros, pipeline_mode=pl.Buffered(1))


def _params(n_axes, flags=None):
    return pltpu.CompilerParams(dimension_semantics=("arbitrary",) * n_axes,
                                vmem_limit_bytes=VMEM_LIMIT, flags=flags)


def _proj_kernel(x_ref, ctab_ref, stab_ref, cos_t_ref, sin_t_ref, ln_g_ref, w_in_ref, qn_g_ref, w_q_t_ref,
                 kvn_g_ref, w_ukv_ref, w_uv_t_ref, w_cdft_ref, head_sel_ref,
                 q_t_ref, k_ref, v_t_ref, kn2_ref, qn2_ref, lb_ref, vr_ref, vi_ref):
    scale = (QK_NOPE_DIM + QK_ROPE_DIM) ** -0.5 * LOG2_E
    half = QK_ROPE_DIM // 2
    qk_dim = QK_NOPE_DIM + QK_ROPE_DIM
    h = _rms(x_ref[0], ln_g_ref[...]).astype(BF16)
    z = _dot(h, w_in_ref[...])

    cq = _rms(z[:, _O_CQ:_O_CQ + Q_LORA_RANK], qn_g_ref[...]).astype(BF16)
    q_t = _dot_nt(w_q_t_ref[...], cq)
    cos_t = cos_t_ref[...] * scale
    sin_t = sin_t_ref[...] * scale
    zero_rows = jnp.zeros((HEAD_PAD - qk_dim, q_t.shape[1]), BF16)
    q_heads = []
    for hd in range(MLA_HEADS):
        src, dst = hd * qk_dim, hd * HEAD_PAD
        x1 = q_t[src + QK_NOPE_DIM:src + QK_NOPE_DIM + half, :]
        x2 = q_t[src + QK_NOPE_DIM + half:src + qk_dim, :]
        q_h = jnp.concatenate([q_t[src:src + QK_NOPE_DIM, :] * scale, x1 * cos_t - x2 * sin_t,
                               x2 * cos_t + x1 * sin_t], axis=0)
        q_heads.append(jnp.concatenate([q_h.astype(BF16), zero_rows], axis=0))
        q_t_ref[0, dst:dst + HEAD_PAD, :] = q_heads[hd]
        qn2_ref[0, hd:hd + 1, :] = jnp.sum(q_h * q_h, axis=0, keepdims=True)

    ckv = _rms(z[:, _O_CKV:_O_CKV + KV_LORA_RANK], kvn_g_ref[...]).astype(BF16)
    kv = _dot(ckv, w_ukv_ref[...])
    zk = z[:, _O_KR:_O_KR + HEAD_PAD]
    kr = zk * ctab_ref[...] + pltpu.roll(zk, HEAD_PAD - QK_ROPE_DIM, 1) * stab_ref[...]
    nope = lax.broadcasted_iota(jnp.int32, kr.shape, 1) < QK_NOPE_DIM
    k_heads = [jnp.where(nope, kv[:, hd * HEAD_PAD:(hd + 1) * HEAD_PAD], kr) for hd in range(MLA_HEADS)]
    for hd in range(MLA_HEADS):
        k_ref[0, hd] = k_heads[hd].astype(BF16)
    k_all = jnp.concatenate(k_heads, axis=1)
    kn2_ref[0] = _dot((k_all * k_all).astype(BF16), head_sel_ref[...])

    pilot_k = k_all[:ATTN_PILOT_KEYS, :].astype(BF16)
    for hd in range(MLA_HEADS):
        pilot = _dot(pilot_k[:, hd * HEAD_PAD:(hd + 1) * HEAD_PAD], q_heads[hd])
        lb_ref[0, hd:hd + 1, :] = jnp.max(pilot, axis=0, keepdims=True)
    v_t_ref[0] = _dot_nt(w_uv_t_ref[...], ckv).astype(BF16)

    for gi in range(FNET_GROUPS):
        sl = slice(gi * FNET_GROUP_DIM, (gi + 1) * FNET_GROUP_DIM)
        v = _dot(z[:, _O_U + gi * FNET_GROUP_DIM:_O_U + (gi + 1) * FNET_GROUP_DIM].astype(BF16), w_cdft_ref[...])
        vr_ref[0, :, sl] = v[:, :FNET_GROUP_DIM].astype(BF16)
        vi_ref[0, :, sl] = v[:, FNET_GROUP_DIM:].astype(BF16)


def _proj(x, ctab, stab, cos_t, sin_t, ln_g, w_in_ext, qn_g, w_q_t, kvn_g, w_ukv, w_uv_t, w_cdft, tile):
    B, S, _ = x.shape
    hp = MLA_HEADS * HEAD_PAD
    tok = lambda w: pl.BlockSpec((1, tile, w), lambda b, t: (b, t, 0))
    tok_t = lambda w: pl.BlockSpec((1, w, tile), lambda b, t: (b, 0, t))
    tab = pl.BlockSpec((tile, LANES), lambda b, t: (t, 0))
    tab_t = pl.BlockSpec((cos_t.shape[0], tile), lambda b, t: (0, t))
    head_sel = jnp.asarray(np.arange(hp)[:, None] // HEAD_PAD == np.arange(LANES)[None, :], BF16)
    return pl.pallas_call(
        _proj_kernel,
        grid=(B, S // tile),
        in_specs=[tok(D_MODEL), tab, tab, tab_t, tab_t, _const_spec(ln_g.shape), _const_spec(w_in_ext.shape),
                  _const_spec(qn_g.shape), _const_spec(w_q_t.shape), _const_spec(kvn_g.shape),
                  _const_spec(w_ukv.shape), _const_spec(w_uv_t.shape), _const_spec(w_cdft.shape),
                  _const_spec(head_sel.shape)],
        out_specs=[tok_t(hp), pl.BlockSpec((1, MLA_HEADS, tile, HEAD_PAD), lambda b, t: (b, 0, t, 0)),
                   tok_t(MLA_WIDTH), tok(LANES), tok_t(MLA_HEADS), tok_t(MLA_HEADS),
                   tok(FNET_WIDTH), tok(FNET_WIDTH)],
        out_shape=[jax.ShapeDtypeStruct((B, hp, S), BF16), jax.ShapeDtypeStruct((B, MLA_HEADS, S, HEAD_PAD), BF16),
                   jax.ShapeDtypeStruct((B, MLA_WIDTH, S), BF16), jax.ShapeDtypeStruct((B, S, LANES), F32),
                   jax.ShapeDtypeStruct((B, MLA_HEADS, S), F32), jax.ShapeDtypeStruct((B, MLA_HEADS, S), F32)]
                  + [jax.ShapeDtypeStruct((B, S, FNET_WIDTH), BF16)] * 2,
        compiler_params=_params(2),
        name="proj",
    )(x, ctab, stab, cos_t, sin_t, ln_g, w_in_ext, qn_g, w_q_t, kvn_g, w_ukv, w_uv_t, w_cdft, head_sel)


def _attn_kernel(safe_ref, q_t_ref, k_ref, v_t_ref, shift_ref, o_ref, s_scr):
    S = k_ref.shape[2]
    kc = min(ATTN_KEY_CHUNK, S)
    n_chunks = S // kc
    heads = range(MLA_HEADS)

    def k_blk(hh, r0, r1):
        return k_ref[0, hh, r0:r1, :]

    def q_blk(hh):
        return q_t_ref[0, hh * HEAD_PAD:(hh + 1) * HEAD_PAD, :]

    def v_blk(hh, c):
        return v_t_ref[0, hh * V_HEAD_DIM:(hh + 1) * V_HEAD_DIM, c * kc:(c + 1) * kc]

    def finish(acc, den):
        o_ref[0] = jnp.concatenate([a / l for a, l in zip(acc, den)], axis=0).T.astype(BF16)

    all_safe = safe_ref[pl.program_id(0), pl.program_id(1)] != 0

    @pl.when(all_safe)
    def _single_pass():
        items = [(hh, c) for hh in heads for c in range(n_chunks)]
        score = lambda hh, c: _dot(k_blk(hh, c * kc, (c + 1) * kc), q_blk(hh))
        acc, den = [None] * len(heads), [None] * len(heads)
        s_next = score(*items[0])
        for i, (hh, c) in enumerate(items):
            s = s_next
            if i + 1 < len(items):
                s_next = score(*items[i + 1])
            p = jnp.exp2(s - shift_ref[0, hh:hh + 1, :])
            o_c, l_c = _dot(v_blk(hh, c), p.astype(BF16)), jnp.sum(p, axis=0, keepdims=True)
            acc[hh] = o_c if acc[hh] is None else acc[hh] + o_c
            den[hh] = l_c if den[hh] is None else den[hh] + l_c
        finish(acc, den)

    @pl.when(jnp.logical_not(all_safe))
    def _two_pass():
        def score_chunk(hh, c):
            s = _dot(k_blk(hh, c * kc, (c + 1) * kc), q_blk(hh))
            s_scr[hh % 2, c * kc:(c + 1) * kc, :] = s
            return jnp.max(s, axis=0, keepdims=True)

        items = [(hh, c) for hh in heads for c in range(n_chunks)]
        m_parts = [[] for _ in heads]
        m_head, acc, den = [None] * len(heads), [None] * len(heads), [None] * len(heads)
        for i in range(len(items) + n_chunks):
            if i < len(items):
                hh, c = items[i]
                m_parts[hh].append(score_chunk(hh, c))
                if c == n_chunks - 1:
                    m_head[hh] = functools.reduce(jnp.maximum, m_parts[hh])
            if i >= n_chunks:
                hh, c = items[i - n_chunks]
                p = jnp.exp2(s_scr[hh % 2, c * kc:(c + 1) * kc, :] - m_head[hh])
                o_c, l_c = _dot(v_blk(hh, c), p.astype(BF16)), jnp.sum(p, axis=0, keepdims=True)
                acc[hh] = o_c if acc[hh] is None else acc[hh] + o_c
                den[hh] = l_c if den[hh] is None else den[hh] + l_c
        finish(acc, den)


def _score_shifts(kn2, qn2, lb, tile):
    B, _, S = lb.shape
    kmax2 = jnp.max(kn2[:, :, :MLA_HEADS], axis=1)
    ub = jnp.sqrt(qn2 * kmax2[:, :, None]) * ATTN_NORM_SLACK
    ok = (ub - lb) <= ATTN_SAFE_GAP
    safe = jnp.all(ok.reshape(B, MLA_HEADS, S // tile, tile), axis=(1, 3))
    return lb + ATTN_SHIFT_MARGIN, safe.astype(jnp.int32)


def _attn(q_t, k, v_t, shift, safe, tile):
    B, _, S, _ = k.shape
    hp = MLA_HEADS * HEAD_PAD
    return pl.pallas_call(
        _attn_kernel,
        grid_spec=pltpu.PrefetchScalarGridSpec(
            num_scalar_prefetch=1,
            grid=(B, S // tile),
            in_specs=[pl.BlockSpec((1, hp, tile), lambda b, t, safe: (b, 0, t)),
                      pl.BlockSpec((1, MLA_HEADS, S, HEAD_PAD), lambda b, t, safe: (b, 0, 0, 0)),
                      pl.BlockSpec((1, MLA_WIDTH, S), lambda b, t, safe: (b, 0, 0)),
                      pl.BlockSpec((1, MLA_HEADS, tile), lambda b, t, safe: (b, 0, t))],
            out_specs=pl.BlockSpec((1, tile, MLA_WIDTH), lambda b, t, safe: (b, t, 0)),
            scratch_shapes=[pltpu.VMEM((2, S, tile), F32)]),
        out_shape=jax.ShapeDtypeStruct((B, S, MLA_WIDTH), BF16),
        compiler_params=_params(2),
        name="attn",
    )(safe, q_t, k, v_t, shift)


def _bit_reverse(i, bits):
    return int(format(i, "0%db" % bits)[::-1], 2)


def _dft_kernel(vr_ref, vi_ref, twc_ref, tws_ref, cs_ref, w_f_ref, g_ref, xr_scr, xi_scr, xb_scr):
    S = vr_ref.shape[1]
    n_leaves = 2 ** DFT_LEVELS
    leaf = S // n_leaves
    chunk = min(DFT_ROW_CHUNK, leaf)
    reps = FNET_WIDTH // LANES
    n, off = S, 0
    for lev in range(DFT_LEVELS):
        half = n // 2
        last = lev == DFT_LEVELS - 1
        for base in range(0, S, n):
            for c0 in range(0, half, chunk):
                top = slice(base + c0, base + c0 + chunk)
                bot = slice(base + half + c0, base + half + c0 + chunk)
                if lev == 0:
                    tr, ti = vr_ref[0, top, :].astype(F32), vi_ref[0, top, :].astype(F32)
                    br, bi = vr_ref[0, bot, :].astype(F32), vi_ref[0, bot, :].astype(F32)
                else:
                    tr, ti, br, bi = xr_scr[top, :], xi_scr[top, :], xr_scr[bot, :], xi_scr[bot, :]
                c = jnp.concatenate([twc_ref[off + c0:off + c0 + chunk, :]] * reps, axis=1)
                s = jnp.concatenate([tws_ref[off + c0:off + c0 + chunk, :]] * reps, axis=1)
                dr, di = tr - br, ti - bi
                ar, ai = tr + br, ti + bi
                mr, mi = dr * c + di * s, di * c - dr * s
                if last:
                    blk, r0 = (base // n) * 2, c0
                    xb_scr[blk, r0:r0 + chunk, :] = ar.astype(BF16)
                    xb_scr[blk, leaf + r0:leaf + r0 + chunk, :] = ai.astype(BF16)
                    xb_scr[blk + 1, r0:r0 + chunk, :] = mr.astype(BF16)
                    xb_scr[blk + 1, leaf + r0:leaf + r0 + chunk, :] = mi.astype(BF16)
                else:
                    xr_scr[top, :], xi_scr[top, :] = ar, ai
                    xr_scr[bot, :], xi_scr[bot, :] = mr, mi
        off += half
        n = half
    for blk in range(n_leaves):
        f = _dot(cs_ref[...], xb_scr[blk])
        g = _dot(f.astype(BF16), w_f_ref[...])
        g_ref[0, _bit_reverse(blk, DFT_LEVELS)] = g.astype(BF16)


def _dft(vr, vi, twc, tws, cs, w_f):
    B, S, _ = vr.shape
    n_leaves = 2 ** DFT_LEVELS
    leaf = S // n_leaves
    bat = pl.BlockSpec((1, S, FNET_WIDTH), lambda b: (b, 0, 0))
    return pl.pallas_call(
        _dft_kernel,
        grid=(B,),
        in_specs=[bat, bat, _const_spec(twc.shape), _const_spec(tws.shape), _const_spec(cs.shape),
                  _const_spec(w_f.shape)],
        out_specs=pl.BlockSpec((1, n_leaves, leaf, FNET_WIDTH), lambda b: (b, 0, 0, 0)),
        out_shape=jax.ShapeDtypeStruct((B, n_leaves, leaf, FNET_WIDTH), BF16),
        scratch_shapes=[pltpu.VMEM((S, FNET_WIDTH), F32), pltpu.VMEM((S, FNET_WIDTH), F32),
                        pltpu.VMEM((n_leaves, 2 * leaf, FNET_WIDTH), BF16)],
        compiler_params=_params(1),
        name="dft",
    )(vr, vi, twc, tws, cs, w_f)


def _post_kernel(x_ref, o_ref, g_ref, mem_ref, w_out_ref, ln_x_g_ref, mem_g_ref, w_xq_ref, w_xkv_ref,
                 w_xo_ref, ln_f_g_ref, w_gu_ref, w_d_ref, fin_g_ref, y_ref, kv_scr, a_scr, g_scr):
    @pl.when(pl.program_id(1) == 0)
    def _():
        m = _rms(mem_ref[0], mem_g_ref[...]).astype(BF16)
        kv_scr[...] = _dot(m, w_xkv_ref[...]).astype(BF16)

    n_res, rows_per_res = g_ref.shape[1], g_ref.shape[2]
    for r in range(n_res):
        g_r = g_ref[0, r].astype(F32)
        for lt in range(FNET_WIDTH // LANES):
            g_scr[lt, pl.ds(r, rows_per_res, stride=n_res), :] = g_r[:, lt * LANES:(lt + 1) * LANES]
    g = jnp.concatenate([g_scr[lt] for lt in range(FNET_WIDTH // LANES)], axis=1).astype(BF16)

    x1 = (x_ref[0] + _dot(o_ref[0], w_out_ref[:MLA_WIDTH, :])
          + _dot(g, w_out_ref[MLA_WIDTH:, :]))

    hq = _rms(x1, ln_x_g_ref[...]).astype(BF16)
    q = (_dot(hq, w_xq_ref[...]) * (XATTN_HEAD_DIM ** -0.5)).astype(BF16)
    heads = []
    for hd in range(XATTN_HEADS):
        sl = slice(hd * XATTN_HEAD_DIM, (hd + 1) * XATTN_HEAD_DIM)
        s = _dot_nt(q[:, sl], kv_scr[:, sl])
        m = jnp.max(s, axis=-1, keepdims=True)
        p = jnp.exp(s - m)
        l = jnp.sum(p, axis=-1, keepdims=True)
        vh = kv_scr[:, D_MODEL + hd * XATTN_HEAD_DIM:D_MODEL + (hd + 1) * XATTN_HEAD_DIM]
        heads.append((_dot(p.astype(BF16), vh) / l).astype(BF16))
    x2 = x1 + _dot(jnp.concatenate(heads, axis=-1), w_xo_ref[...])

    hf = _rms(x2, ln_f_g_ref[...]).astype(BF16)
    for c in range(D_FF // FF_CHUNK):
        sl = slice(c * FF_CHUNK, (c + 1) * FF_CHUNK)
        gate = _dot(hf, w_gu_ref[:, sl])
        up = _dot(hf, w_gu_ref[:, D_FF + c * FF_CHUNK:D_FF + (c + 1) * FF_CHUNK])
        a_scr[:, sl] = (gate * jax.nn.sigmoid(gate) * up).astype(BF16)
    x3 = x2 + _dot(a_scr[...], w_d_ref[...])
    y_ref[0] = _rms(x3, fin_g_ref[...])


def _post(x, o, g, mem, w_out, ln_x_g, mem_g, w_xq, w_xkv, w_xo, ln_f_g, w_gu, w_d, fin_g, tile):
    B, S, _ = x.shape
    M = mem.shape[1]
    n_res = g.shape[1]
    tok = lambda w: pl.BlockSpec((1, tile, w), lambda b, t: (b, t, 0))
    consts = [w_out, ln_x_g, mem_g, w_xq, w_xkv, w_xo, ln_f_g, w_gu, w_d, fin_g]
    return pl.pallas_call(
        _post_kernel,
        grid=(B, S // tile),
        in_specs=[tok(D_MODEL), tok(MLA_WIDTH),
                  pl.BlockSpec((1, n_res, tile // n_res, FNET_WIDTH), lambda b, t: (b, 0, t, 0)),
                  pl.BlockSpec((1, M, D_MODEL), lambda b, t: (b, 0, 0))]
                 + [_const_spec(c.shape) for c in consts],
        out_specs=tok(D_MODEL),
        out_shape=jax.ShapeDtypeStruct((B, S, D_MODEL), F32),
        scratch_shapes=[pltpu.VMEM((M, 2 * D_MODEL), BF16), pltpu.VMEM((tile, D_FF), BF16),
                        pltpu.VMEM((FNET_WIDTH // LANES, tile, LANES), F32)],
        compiler_params=_params(2),
        name="post",
    )(x, o, g, mem, *consts)


def _rot_cols(w):
    half = QK_ROPE_DIM // 2
    return jnp.concatenate([-w[..., half:], w[..., :half]], axis=-1)


def _prep_weights(w_in, w_uq, w_ukv, w_fnet):
    w_cq = w_in[:, :Q_LORA_RANK]
    w_ckv = w_in[:, Q_LORA_RANK:Q_LORA_RANK + KV_LORA_RANK]
    o3 = Q_LORA_RANK + KV_LORA_RANK
    w_kr = w_in[:, o3:o3 + QK_ROPE_DIM]
    w_u = w_in[:, o3 + QK_ROPE_DIM:]
    w_kr_grp = jnp.concatenate([jnp.zeros((D_MODEL, QK_NOPE_DIM), F32), w_kr, _rot_cols(w_kr)], axis=-1)
    w_in_ext = jnp.concatenate([w_cq, w_ckv, w_kr_grp, w_u], axis=-1).astype(BF16)

    w_q_t = w_uq.T.astype(BF16)

    kv_w = w_ukv.reshape(KV_LORA_RANK, MLA_HEADS, QK_NOPE_DIM + V_HEAD_DIM)
    w_uv_t = kv_w[..., QK_NOPE_DIM:].reshape(KV_LORA_RANK, MLA_WIDTH).T.astype(BF16)

    w_f = jnp.zeros((FNET_WIDTH, FNET_WIDTH), F32)
    for gi in range(FNET_GROUPS):
        sl = slice(gi * FNET_GROUP_DIM, (gi + 1) * FNET_GROUP_DIM)
        w_f = w_f.at[sl, sl].set(w_fnet[gi])
    return w_in_ext, w_q_t, w_uv_t, w_f.astype(BF16)


def _dft_mats(n):
    idx = np.arange(n, dtype=np.int64)
    ang = ((idx[:, None] * idx[None, :]) % n) * (2.0 * np.pi / n)
    return np.cos(ang), np.sin(ang)


def _seq_dft_tables(seq):
    cs_rows, sn_rows = [], []
    n = seq
    for _ in range(DFT_LEVELS):
        ang = np.arange(n // 2) * (2.0 * np.pi / n)
        cs_rows.append(np.cos(ang))
        sn_rows.append(np.sin(ang))
        n //= 2
    rep = lambda rows: jnp.asarray(np.broadcast_to(np.concatenate(rows)[:, None], (seq - n, LANES)), F32)
    c, s = _dft_mats(n)
    leaf = jnp.asarray(np.concatenate([c, s], axis=1) * seq ** -0.5, F32).astype(BF16)
    return rep(cs_rows), rep(sn_rows), leaf


def _channel_dft():
    c, s = _dft_mats(FNET_GROUP_DIM)
    return jnp.asarray(np.concatenate([c, -s], axis=-1) * FNET_GROUP_DIM ** -0.5, F32).astype(BF16)


def _rope_tabs(seq):
    inv = 1.0 / (ROPE_BASE ** (np.arange(0, QK_ROPE_DIM, 2) / QK_ROPE_DIM))
    ang = np.arange(seq)[:, None] * inv[None, :]
    cos, sin = np.cos(ang), np.sin(ang)
    pad = lambda t: jnp.asarray(np.concatenate(
        [np.zeros((seq, QK_NOPE_DIM)), t, t, np.zeros((seq, HEAD_PAD - QK_NOPE_DIM - QK_ROPE_DIM))], axis=-1), F32)
    return pad(cos), pad(sin), jnp.asarray(cos.T, F32), jnp.asarray(sin.T, F32)


def _trunk(x, mem, w, tiles):
    S = x.shape[1]
    ctab, stab, cos_t, sin_t = _rope_tabs(S)
    twc, tws, cs = _seq_dft_tables(S)
    q_t, k, v_t, kn2, qn2, lb, vr, vi = _proj(x, ctab, stab, cos_t, sin_t, w["ln_mix_g"], w["w_in_ext"],
                                              w["q_norm_g"], w["w_q_t"], w["kv_norm_g"], w["w_ukv"],
                                              w["w_uv_t"], w["w_cdft"], min(tiles[0], S))
    shift, safe = _score_shifts(kn2, qn2, lb, min(tiles[1], S))
    o = _attn(q_t, k, v_t, shift, safe, min(tiles[1], S))
    g = _dft(vr, vi, twc, tws, cs, w["w_f"])
    return _post(x, o, g, mem, w["w_out"], w["ln_x_g"], w["mem_norm_g"], w["w_xq"], w["w_xkv"], w["w_xo"],
                 w["ln_ffn_g"], w["w_gate_up"], w["w_down"], w["final_norm_g"], min(tiles[2], S))


def kernel(x_prompt, x_sample, mem_prompt, mem_sample, ln_mix_g, w_in, q_norm_g, w_uq, kv_norm_g, w_ukv, w_fnet,
           w_out, ln_x_g, mem_norm_g, w_xq, w_xkv, w_xo, ln_ffn_g, w_gate_up, w_down, final_norm_g):
    assert ln_mix_g.shape[0] == 1, "single-layer trunk"
    w_in_ext, w_q_t, w_uv_t, w_f = _prep_weights(w_in[0], w_uq[0], w_ukv[0], w_fnet[0])
    row = lambda g: g.reshape(1, -1).astype(F32)
    w = dict(
        ln_mix_g=row(ln_mix_g[0]), w_in_ext=w_in_ext, q_norm_g=row(q_norm_g[0]), w_q_t=w_q_t,
        kv_norm_g=row(kv_norm_g[0]), w_ukv=w_ukv[0].astype(BF16), w_uv_t=w_uv_t, w_cdft=_channel_dft(), w_f=w_f,
        w_out=w_out[0].astype(BF16), ln_x_g=row(ln_x_g[0]), mem_norm_g=row(mem_norm_g[0]),
        w_xq=w_xq[0].astype(BF16), w_xkv=w_xkv[0].astype(BF16), w_xo=w_xo[0].astype(BF16),
        ln_ffn_g=row(ln_ffn_g[0]), w_gate_up=w_gate_up[0].astype(BF16), w_down=w_down[0].astype(BF16),
        final_norm_g=row(final_norm_g),
    )
    return (_trunk(x_prompt, mem_prompt, w, TILES), _trunk(x_sample, mem_sample, w, TILES))
```

```python
import functools

import numpy as np
import jax
import jax.numpy as jnp
from jax import lax
from jax.experimental import pallas as pl
from jax.experimental.pallas import tpu as pltpu

F32 = jnp.float32
BF16 = jnp.bfloat16

D_MODEL = 1024
MLA_HEADS = 8
QK_NOPE_DIM = 64
QK_ROPE_DIM = 32
V_HEAD_DIM = 64
Q_LORA_RANK = 384
KV_LORA_RANK = 256
FNET_GROUPS = 4
FNET_GROUP_DIM = 128
FNET_WIDTH = FNET_GROUPS * FNET_GROUP_DIM
MLA_WIDTH = MLA_HEADS * V_HEAD_DIM
XATTN_HEADS = 4
XATTN_HEAD_DIM = D_MODEL // XATTN_HEADS
D_FF = 2816
ROPE_BASE = 10000.0
NORM_EPS = 1e-6
LOG2_E = 1.4426950408889634

LANES = 128
HEAD_PAD = LANES
FF_CHUNK = 256
VMEM_LIMIT = 56 * 1024 * 1024
TILES = (1024, 512, 512)
ATTN_KEY_CHUNK = 1024
ATTN_PILOT_KEYS = 64
ATTN_SHIFT_MARGIN = 30.0
ATTN_SAFE_GAP = 90.0
ATTN_NORM_SLACK = 1.02
DFT_LEVELS = 3
DFT_ROW_CHUNK = 256

_O_CQ = 0
_O_CKV = _O_CQ + Q_LORA_RANK
_O_KR = _O_CKV + KV_LORA_RANK
_O_U = _O_KR + HEAD_PAD
IN_EXT = _O_U + FNET_WIDTH


def _rms(x, g):
    return x * lax.rsqrt(jnp.mean(x * x, axis=-1, keepdims=True) + NORM_EPS) * g


def _dot(a, b):
    return jnp.dot(a, b, preferred_element_type=F32)


def _dot_nt(a, b):
    return lax.dot_general(a, b, (((1,), (1,)), ((), ())), preferred_element_type=F32)


def _const_spec(shape):
    zeros = (0,) * len(shape)
    return pl.BlockSpec(shape, lambda *_: zeros, pipeline_mode=pl.Buffered(1))


def _params(n_axes, flags=None):
    return pltpu.CompilerParams(dimension_semantics=("arbitrary",) * n_axes,
                                vmem_limit_bytes=VMEM_LIMIT, flags=flags)


def _proj_kernel(x_ref, ctab_ref, stab_ref, cos_t_ref, sin_t_ref, ln_g_ref, w_in_ref, qn_g_ref, w_q_t_ref,
                 kvn_g_ref, w_ukv_ref, w_uv_t_ref, w_cdft_ref, head_sel_ref,
                 q_t_ref, k_ref, v_t_ref, kn2_ref, qn2_ref, lb_ref, vr_ref, vi_ref):
    scale = (QK_NOPE_DIM + QK_ROPE_DIM) ** -0.5 * LOG2_E
    half = QK_ROPE_DIM // 2
    qk_dim = QK_NOPE_DIM + QK_ROPE_DIM
    h = _rms(x_ref[0], ln_g_ref[...]).astype(BF16)
    z = _dot(h, w_in_ref[...])

    cq = _rms(z[:, _O_CQ:_O_CQ + Q_LORA_RANK], qn_g_ref[...]).astype(BF16)
    q_t = _dot_nt(w_q_t_ref[...], cq)
    cos_t = cos_t_ref[...] * scale
    sin_t = sin_t_ref[...] * scale
    zero_rows = jnp.zeros((HEAD_PAD - qk_dim, q_t.shape[1]), BF16)
    q_heads = []
    for hd in range(MLA_HEADS):
        src, dst = hd * qk_dim, hd * HEAD_PAD
        x1 = q_t[src + QK_NOPE_DIM:src + QK_NOPE_DIM + half, :]
        x2 = q_t[src + QK_NOPE_DIM + half:src + qk_dim, :]
        q_h = jnp.concatenate([q_t[src:src + QK_NOPE_DIM, :] * scale, x1 * cos_t - x2 * sin_t,
                               x2 * cos_t + x1 * sin_t], axis=0)
        q_heads.append(jnp.concatenate([q_h.astype(BF16), zero_rows], axis=0))
        q_t_ref[0, dst:dst + HEAD_PAD, :] = q_heads[hd]
        qn2_ref[0, hd:hd + 1, :] = jnp.sum(q_h * q_h, axis=0, keepdims=True)

    ckv = _rms(z[:, _O_CKV:_O_CKV + KV_LORA_RANK], kvn_g_ref[...]).astype(BF16)
    kv = _dot(ckv, w_ukv_ref[...])
    zk = z[:, _O_KR:_O_KR + HEAD_PAD]
    kr = zk * ctab_ref[...] + pltpu.roll(zk, HEAD_PAD - QK_ROPE_DIM, 1) * stab_ref[...]
    nope = lax.broadcasted_iota(jnp.int32, kr.shape, 1) < QK_NOPE_DIM
    k_heads = [jnp.where(nope, kv[:, hd * HEAD_PAD:(hd + 1) * HEAD_PAD], kr) for hd in range(MLA_HEADS)]
    for hd in range(MLA_HEADS):
        k_ref[0, hd] = k_heads[hd].astype(BF16)
    k_all = jnp.concatenate(k_heads, axis=1)
    kn2 = _dot((k_all * k_all).astype(BF16), head_sel_ref[...])
    kn2_ref[0, 0] = jnp.broadcast_to(jnp.max(kn2, axis=0, keepdims=True), kn2_ref.shape[2:])

    pilot_k = k_all[:ATTN_PILOT_KEYS, :].astype(BF16)
    for hd in range(MLA_HEADS):
        pilot = _dot(pilot_k[:, hd * HEAD_PAD:(hd + 1) * HEAD_PAD], q_heads[hd])
        lb_ref[0, hd:hd + 1, :] = jnp.max(pilot, axis=0, keepdims=True)
    v_t_ref[0] = _dot_nt(w_uv_t_ref[...], ckv).astype(BF16)

    for gi in range(FNET_GROUPS):
        sl = slice(gi * FNET_GROUP_DIM, (gi + 1) * FNET_GROUP_DIM)
        v = _dot(z[:, _O_U + gi * FNET_GROUP_DIM:_O_U + (gi + 1) * FNET_GROUP_DIM].astype(BF16), w_cdft_ref[...])
        vr_ref[0, :, sl] = v[:, :FNET_GROUP_DIM].astype(BF16)
        vi_ref[0, :, sl] = v[:, FNET_GROUP_DIM:].astype(BF16)


def _proj(x, ctab, stab, cos_t, sin_t, ln_g, w_in_ext, qn_g, w_q_t, kvn_g, w_ukv, w_uv_t, w_cdft, tile):
    B, S, _ = x.shape
    hp = MLA_HEADS * HEAD_PAD
    tok = lambda w: pl.BlockSpec((1, tile, w), lambda b, t: (b, t, 0))
    tok_t = lambda w: pl.BlockSpec((1, w, tile), lambda b, t: (b, 0, t))
    tab = pl.BlockSpec((tile, LANES), lambda b, t: (t, 0))
    tab_t = pl.BlockSpec((cos_t.shape[0], tile), lambda b, t: (0, t))
    head_sel = jnp.asarray(np.arange(hp)[:, None] // HEAD_PAD == np.arange(LANES)[None, :], BF16)
    return pl.pallas_call(
        _proj_kernel,
        grid=(B, S // tile),
        in_specs=[tok(D_MODEL), tab, tab, tab_t, tab_t, _const_spec(ln_g.shape), _const_spec(w_in_ext.shape),
                  _const_spec(qn_g.shape), _const_spec(w_q_t.shape), _const_spec(kvn_g.shape),
                  _const_spec(w_ukv.shape), _const_spec(w_uv_t.shape), _const_spec(w_cdft.shape),
                  _const_spec(head_sel.shape)],
        out_specs=[tok_t(hp), pl.BlockSpec((1, MLA_HEADS, tile, HEAD_PAD), lambda b, t: (b, 0, t, 0)),
                   tok_t(MLA_WIDTH), pl.BlockSpec((1, 1, 8, LANES), lambda b, t: (b, t, 0, 0)),
                   tok_t(MLA_HEADS), tok_t(MLA_HEADS),
                   tok(FNET_WIDTH), tok(FNET_WIDTH)],
        out_shape=[jax.ShapeDtypeStruct((B, hp, S), BF16), jax.ShapeDtypeStruct((B, MLA_HEADS, S, HEAD_PAD), BF16),
                   jax.ShapeDtypeStruct((B, MLA_WIDTH, S), BF16), jax.ShapeDtypeStruct((B, S // tile, 8, LANES), F32),
                   jax.ShapeDtypeStruct((B, MLA_HEADS, S), F32), jax.ShapeDtypeStruct((B, MLA_HEADS, S), F32)]
                  + [jax.ShapeDtypeStruct((B, S, FNET_WIDTH), BF16)] * 2,
        compiler_params=_params(2),
        name="proj",
    )(x, ctab, stab, cos_t, sin_t, ln_g, w_in_ext, qn_g, w_q_t, kvn_g, w_ukv, w_uv_t, w_cdft, head_sel)


def _attn_kernel(safe_ref, q_t_ref, k_ref, v_t_ref, shift_ref, o_ref, s_scr):
    S = k_ref.shape[2]
    kc = min(ATTN_KEY_CHUNK, S)
    n_chunks = S // kc
    heads = range(MLA_HEADS)

    def k_blk(hh, r0, r1):
        return k_ref[0, hh, r0:r1, :]

    def q_blk(hh):
        return q_t_ref[0, hh * HEAD_PAD:(hh + 1) * HEAD_PAD, :]

    def v_blk(hh, c):
        return v_t_ref[0, hh * V_HEAD_DIM:(hh + 1) * V_HEAD_DIM, c * kc:(c + 1) * kc]

    def finish(acc, den):
        o_ref[0] = jnp.concatenate([a / l for a, l in zip(acc, den)], axis=0).T.astype(BF16)

    all_safe = safe_ref[pl.program_id(0), pl.program_id(1)] != 0

    @pl.when(all_safe)
    def _single_pass():
        items = [(hh, c) for hh in heads for c in range(n_chunks)]
        score = lambda hh, c: _dot(k_blk(hh, c * kc, (c + 1) * kc), q_blk(hh))
        acc, den = [None] * len(heads), [None] * len(heads)
        s_next = score(*items[0])
        for i, (hh, c) in enumerate(items):
            s = s_next
            if i + 1 < len(items):
                s_next = score(*items[i + 1])
            p = jnp.exp2(s - shift_ref[0, hh:hh + 1, :])
            o_c, l_c = _dot(v_blk(hh, c), p.astype(BF16)), jnp.sum(p, axis=0, keepdims=True)
            acc[hh] = o_c if acc[hh] is None else acc[hh] + o_c
            den[hh] = l_c if den[hh] is None else den[hh] + l_c
        finish(acc, den)

    @pl.when(jnp.logical_not(all_safe))
    def _two_pass():
        def score_chunk(hh, c):
            s = _dot(k_blk(hh, c * kc, (c + 1) * kc), q_blk(hh))
            s_scr[hh % 2, c * kc:(c + 1) * kc, :] = s
            return jnp.max(s, axis=0, keepdims=True)

        items = [(hh, c) for hh in heads for c in range(n_chunks)]
        m_parts = [[] for _ in heads]
        m_head, acc, den = [None] * len(heads), [None] * len(heads), [None] * len(heads)
        for i in range(len(items) + n_chunks):
            if i < len(items):
                hh, c = items[i]
                m_parts[hh].append(score_chunk(hh, c))
                if c == n_chunks - 1:
                    m_head[hh] = functools.reduce(jnp.maximum, m_parts[hh])
            if i >= n_chunks:
                hh, c = items[i - n_chunks]
                p = jnp.exp2(s_scr[hh % 2, c * kc:(c + 1) * kc, :] - m_head[hh])
                o_c, l_c = _dot(v_blk(hh, c), p.astype(BF16)), jnp.sum(p, axis=0, keepdims=True)
                acc[hh] = o_c if acc[hh] is None else acc[hh] + o_c
                den[hh] = l_c if den[hh] is None else den[hh] + l_c
        finish(acc, den)


def _score_shifts(kn2, qn2, lb, tile):
    B, _, S = lb.shape
    kmax2 = jnp.max(kn2[:, :, 0, :MLA_HEADS], axis=1)
    ub = jnp.sqrt(qn2 * kmax2[:, :, None]) * ATTN_NORM_SLACK
    ok = (ub - lb) <= ATTN_SAFE_GAP
    safe = jnp.all(ok.reshape(B, MLA_HEADS, S // tile, tile), axis=(1, 3))
    return lb + ATTN_SHIFT_MARGIN, safe.astype(jnp.int32)


def _attn(q_t, k, v_t, shift, safe, tile):
    B, _, S, _ = k.shape
    hp = MLA_HEADS * HEAD_PAD
    return pl.pallas_call(
        _attn_kernel,
        grid_spec=pltpu.PrefetchScalarGridSpec(
            num_scalar_prefetch=1,
            grid=(B, S // tile),
            in_specs=[pl.BlockSpec((1, hp, tile), lambda b, t, safe: (b, 0, t)),
                      pl.BlockSpec((1, MLA_HEADS, S, HEAD_PAD), lambda b, t, safe: (b, 0, 0, 0)),
                      pl.BlockSpec((1, MLA_WIDTH, S), lambda b, t, safe: (b, 0, 0)),
                      pl.BlockSpec((1, MLA_HEADS, tile), lambda b, t, safe: (b, 0, t))],
            out_specs=pl.BlockSpec((1, tile, MLA_WIDTH), lambda b, t, safe: (b, t, 0)),
            scratch_shapes=[pltpu.VMEM((2, S, tile), F32)]),
        out_shape=jax.ShapeDtypeStruct((B, S, MLA_WIDTH), BF16),
        compiler_params=_params(2),
        name="attn",
    )(safe, q_t, k, v_t, shift)


def _bit_reverse(i, bits):
    return int(format(i, "0%db" % bits)[::-1], 2)


def _dft_kernel(vr_ref, vi_ref, twc_ref, tws_ref, cs_ref, w_f_ref, g_ref, xr_scr, xi_scr, xb_scr):
    S = vr_ref.shape[1]
    n_leaves = 2 ** DFT_LEVELS
    leaf = S // n_leaves
    chunk = min(DFT_ROW_CHUNK, leaf)
    reps = FNET_WIDTH // LANES
    n, off = S, 0
    for lev in range(DFT_LEVELS):
        half = n // 2
        last = lev == DFT_LEVELS - 1
        for base in range(0, S, n):
            for c0 in range(0, half, chunk):
                top = slice(base + c0, base + c0 + chunk)
                bot = slice(base + half + c0, base + half + c0 + chunk)
                if lev == 0:
                    tr, ti = vr_ref[0, top, :].astype(F32), vi_ref[0, top, :].astype(F32)
                    br, bi = vr_ref[0, bot, :].astype(F32), vi_ref[0, bot, :].astype(F32)
                else:
                    tr, ti, br, bi = xr_scr[top, :], xi_scr[top, :], xr_scr[bot, :], xi_scr[bot, :]
                c = jnp.concatenate([twc_ref[off + c0:off + c0 + chunk, :]] * reps, axis=1)
                s = jnp.concatenate([tws_ref[off + c0:off + c0 + chunk, :]] * reps, axis=1)
                dr, di = tr - br, ti - bi
                ar, ai = tr + br, ti + bi
                mr, mi = dr * c + di * s, di * c - dr * s
                if last:
                    blk, r0 = (base // n) * 2, c0
                    xb_scr[blk, r0:r0 + chunk, :] = ar.astype(BF16)
                    xb_scr[blk, leaf + r0:leaf + r0 + chunk, :] = ai.astype(BF16)
                    xb_scr[blk + 1, r0:r0 + chunk, :] = mr.astype(BF16)
                    xb_scr[blk + 1, leaf + r0:leaf + r0 + chunk, :] = mi.astype(BF16)
                else:
                    xr_scr[top, :], xi_scr[top, :] = ar, ai
                    xr_scr[bot, :], xi_scr[bot, :] = mr, mi
        off += half
        n = half
    for blk in range(n_leaves):
        f = _dot(cs_ref[...], xb_scr[blk])
        g = _dot(f.astype(BF16), w_f_ref[...])
        g_ref[0, _bit_reverse(blk, DFT_LEVELS)] = g.astype(BF16)


def _dft(vr, vi, twc, tws, cs, w_f):
    B, S, _ = vr.shape
    n_leaves = 2 ** DFT_LEVELS
    leaf = S // n_leaves
    bat = pl.BlockSpec((1, S, FNET_WIDTH), lambda b: (b, 0, 0))
    return pl.pallas_call(
        _dft_kernel,
        grid=(B,),
        in_specs=[bat, bat, _const_spec(twc.shape), _const_spec(tws.shape), _const_spec(cs.shape),
                  _const_spec(w_f.shape)],
        out_specs=pl.BlockSpec((1, n_leaves, leaf, FNET_WIDTH), lambda b: (b, 0, 0, 0)),
        out_shape=jax.ShapeDtypeStruct((B, n_leaves, leaf, FNET_WIDTH), BF16),
        scratch_shapes=[pltpu.VMEM((S, FNET_WIDTH), F32), pltpu.VMEM((S, FNET_WIDTH), F32),
                        pltpu.VMEM((n_leaves, 2 * leaf, FNET_WIDTH), BF16)],
        compiler_params=_params(1),
        name="dft",
    )(vr, vi, twc, tws, cs, w_f)


def _post_kernel(x_ref, o_ref, g_ref, mem_ref, w_out_ref, ln_x_g_ref, mem_g_ref, w_xq_ref, w_xkv_ref,
                 w_xo_ref, ln_f_g_ref, w_gu_ref, w_d_ref, fin_g_ref, y_ref, kv_scr, a_scr, g_scr):
    @pl.when(pl.program_id(1) == 0)
    def _():
        m = _rms(mem_ref[0], mem_g_ref[...]).astype(BF16)
        kv_scr[...] = _dot(m, w_xkv_ref[...]).astype(BF16)

    n_res, rows_per_res = g_ref.shape[1], g_ref.shape[2]
    for r in range(n_res):
        g_r = g_ref[0, r].astype(F32)
        for lt in range(FNET_WIDTH // LANES):
            g_scr[lt, pl.ds(r, rows_per_res, stride=n_res), :] = g_r[:, lt * LANES:(lt + 1) * LANES]
    g = jnp.concatenate([g_scr[lt] for lt in range(FNET_WIDTH // LANES)], axis=1).astype(BF16)

    x1 = (x_ref[0] + _dot(o_ref[0], w_out_ref[:MLA_WIDTH, :])
          + _dot(g, w_out_ref[MLA_WIDTH:, :]))

    hq = _rms(x1, ln_x_g_ref[...]).astype(BF16)
    q = (_dot(hq, w_xq_ref[...]) * (XATTN_HEAD_DIM ** -0.5 * LOG2_E)).astype(BF16)
    heads = []
    for hd in range(XATTN_HEADS):
        sl = slice(hd * XATTN_HEAD_DIM, (hd + 1) * XATTN_HEAD_DIM)
        s = _dot_nt(q[:, sl], kv_scr[:, sl])
        m = jnp.max(s, axis=-1, keepdims=True)
        p = jnp.exp2(s - m)
        l = jnp.sum(p, axis=-1, keepdims=True)
        vh = kv_scr[:, D_MODEL + hd * XATTN_HEAD_DIM:D_MODEL + (hd + 1) * XATTN_HEAD_DIM]
        heads.append((_dot(p.astype(BF16), vh) / l).astype(BF16))
    x2 = x1 + _dot(jnp.concatenate(heads, axis=-1), w_xo_ref[...])

    hf = _rms(x2, ln_f_g_ref[...]).astype(BF16)
    for c in range(D_FF // FF_CHUNK):
        sl = slice(c * FF_CHUNK, (c + 1) * FF_CHUNK)
        gate = _dot(hf, w_gu_ref[:, sl])
        up = _dot(hf, w_gu_ref[:, D_FF + c * FF_CHUNK:D_FF + (c + 1) * FF_CHUNK])
        a_scr[:, sl] = (gate * jax.nn.sigmoid(gate) * up).astype(BF16)
    x3 = x2 + _dot(a_scr[...], w_d_ref[...])
    y_ref[0] = _rms(x3, fin_g_ref[...])


def _post(x, o, g, mem, w_out, ln_x_g, mem_g, w_xq, w_xkv, w_xo, ln_f_g, w_gu, w_d, fin_g, tile):
    B, S, _ = x.shape
    M = mem.shape[1]
    n_res = g.shape[1]
    tok = lambda w: pl.BlockSpec((1, tile, w), lambda b, t: (b, t, 0))
    consts = [w_out, ln_x_g, mem_g, w_xq, w_xkv, w_xo, ln_f_g, w_gu, w_d, fin_g]
    return pl.pallas_call(
        _post_kernel,
        grid=(B, S // tile),
        in_specs=[tok(D_MODEL), tok(MLA_WIDTH),
                  pl.BlockSpec((1, n_res, tile // n_res, FNET_WIDTH), lambda b, t: (b, 0, t, 0)),
                  pl.BlockSpec((1, M, D_MODEL), lambda b, t: (b, 0, 0))]
                 + [_const_spec(c.shape) for c in consts],
        out_specs=tok(D_MODEL),
        out_shape=jax.ShapeDtypeStruct((B, S, D_MODEL), F32),
        scratch_shapes=[pltpu.VMEM((M, 2 * D_MODEL), BF16), pltpu.VMEM((tile, D_FF), BF16),
                        pltpu.VMEM((FNET_WIDTH // LANES, tile, LANES), F32)],
        compiler_params=_params(2),
        name="post",
    )(x, o, g, mem, *consts)


def _rot_cols(w):
    half = QK_ROPE_DIM // 2
    return jnp.concatenate([-w[..., half:], w[..., :half]], axis=-1)


def _prep_weights(w_in, w_uq, w_ukv, w_fnet):
    w_cq = w_in[:, :Q_LORA_RANK]
    w_ckv = w_in[:, Q_LORA_RANK:Q_LORA_RANK + KV_LORA_RANK]
    o3 = Q_LORA_RANK + KV_LORA_RANK
    w_kr = w_in[:, o3:o3 + QK_ROPE_DIM]
    w_u = w_in[:, o3 + QK_ROPE_DIM:]
    w_kr_grp = jnp.concatenate([jnp.zeros((D_MODEL, QK_NOPE_DIM), F32), w_kr, _rot_cols(w_kr)], axis=-1)
    w_in_ext = jnp.concatenate([w_cq, w_ckv, w_kr_grp, w_u], axis=-1).astype(BF16)

    w_q_t = w_uq.T.astype(BF16)

    kv_w = w_ukv.reshape(KV_LORA_RANK, MLA_HEADS, QK_NOPE_DIM + V_HEAD_DIM)
    w_uv_t = kv_w[..., QK_NOPE_DIM:].reshape(KV_LORA_RANK, MLA_WIDTH).T.astype(BF16)

    w_f = jnp.zeros((FNET_WIDTH, FNET_WIDTH), F32)
    for gi in range(FNET_GROUPS):
        sl = slice(gi * FNET_GROUP_DIM, (gi + 1) * FNET_GROUP_DIM)
        w_f = w_f.at[sl, sl].set(w_fnet[gi])
    return w_in_ext, w_q_t, w_uv_t, w_f.astype(BF16)


def _dft_mats(n):
    idx = np.arange(n, dtype=np.int64)
    ang = ((idx[:, None] * idx[None, :]) % n) * (2.0 * np.pi / n)
    return np.cos(ang), np.sin(ang)


def _seq_dft_tables(seq):
    cs_rows, sn_rows = [], []
    n = seq
    for _ in range(DFT_LEVELS):
        ang = np.arange(n // 2) * (2.0 * np.pi / n)
        cs_rows.append(np.cos(ang))
        sn_rows.append(np.sin(ang))
        n //= 2
    rep = lambda rows: jnp.asarray(np.broadcast_to(np.concatenate(rows)[:, None], (seq - n, LANES)), F32)
    c, s = _dft_mats(n)
    leaf = jnp.asarray(np.concatenate([c, s], axis=1) * seq ** -0.5, F32).astype(BF16)
    return rep(cs_rows), rep(sn_rows), leaf


def _channel_dft():
    c, s = _dft_mats(FNET_GROUP_DIM)
    return jnp.asarray(np.concatenate([c, -s], axis=-1) * FNET_GROUP_DIM ** -0.5, F32).astype(BF16)


def _rope_tabs(seq):
    inv = 1.0 / (ROPE_BASE ** (np.arange(0, QK_ROPE_DIM, 2) / QK_ROPE_DIM))
    ang = np.arange(seq)[:, None] * inv[None, :]
    cos, sin = np.cos(ang), np.sin(ang)
    pad = lambda t: jnp.asarray(np.concatenate(
        [np.zeros((seq, QK_NOPE_DIM)), t, t, np.zeros((seq, HEAD_PAD - QK_NOPE_DIM - QK_ROPE_DIM))], axis=-1), F32)
    return pad(cos), pad(sin), jnp.asarray(cos.T, F32), jnp.asarray(sin.T, F32)


def _trunk(x, mem, w, tiles):
    S = x.shape[1]
    ctab, stab, cos_t, sin_t = _rope_tabs(S)
    twc, tws, cs = _seq_dft_tables(S)
    q_t, k, v_t, kn2, qn2, lb, vr, vi = _proj(x, ctab, stab, cos_t, sin_t, w["ln_mix_g"], w["w_in_ext"],
                                              w["q_norm_g"], w["w_q_t"], w["kv_norm_g"], w["w_ukv"],
                                              w["w_uv_t"], w["w_cdft"], min(tiles[0], S))
    shift, safe = _score_shifts(kn2, qn2, lb, min(tiles[1], S))
    o = _attn(q_t, k, v_t, shift, safe, min(tiles[1], S))
    g = _dft(vr, vi, twc, tws, cs, w["w_f"])
    return _post(x, o, g, mem, w["w_out"], w["ln_x_g"], w["mem_norm_g"], w["w_xq"], w["w_xkv"], w["w_xo"],
                 w["ln_ffn_g"], w["w_gate_up"], w["w_down"], w["final_norm_g"], min(tiles[2], S))


def kernel(x_prompt, x_sample, mem_prompt, mem_sample, ln_mix_g, w_in, q_norm_g, w_uq, kv_norm_g, w_ukv, w_fnet,
           w_out, ln_x_g, mem_norm_g, w_xq, w_xkv, w_xo, ln_ffn_g, w_gate_up, w_down, final_norm_g):
    assert ln_mix_g.shape[0] == 1, "single-layer trunk"
    w_in_ext, w_q_t, w_uv_t, w_f = _prep_weights(w_in[0], w_uq[0], w_ukv[0], w_fnet[0])
    row = lambda g: g.reshape(1, -1).astype(F32)
    w = dict(
        ln_mix_g=row(ln_mix_g[0]), w_in_ext=w_in_ext, q_norm_g=row(q_norm_g[0]), w_q_t=w_q_t,
        kv_norm_g=row(kv_norm_g[0]), w_ukv=w_ukv[0].astype(BF16), w_uv_t=w_uv_t, w_cdft=_channel_dft(), w_f=w_f,
        w_out=w_out[0].astype(BF16), ln_x_g=row(ln_x_g[0]), mem_norm_g=row(mem_norm_g[0]),
        w_xq=w_xq[0].astype(BF16), w_xkv=w_xkv[0].astype(BF16), w_xo=w_xo[0].astype(BF16),
        ln_ffn_g=row(ln_ffn_g[0]), w_gate_up=w_gate_up[0].astype(BF16), w_down=w_down[0].astype(BF16),
        final_norm_g=row(final_norm_g),
    )
    return (_trunk(x_prompt, mem_prompt, w, TILES), _trunk(x_sample, mem_sample, w, TILES))
```

```python
import functools

import numpy as np
import jax
import jax.numpy as jnp
from jax import lax
from jax.experimental import pallas as pl
from jax.experimental.pallas import tpu as pltpu

F32 = jnp.float32
BF16 = jnp.bfloat16

D_MODEL = 1024
MLA_HEADS = 8
QK_NOPE_DIM = 64
QK_ROPE_DIM = 32
V_HEAD_DIM = 64
Q_LORA_RANK = 384
KV_LORA_RANK = 256
FNET_GROUPS = 4
FNET_GROUP_DIM = 128
FNET_WIDTH = FNET_GROUPS * FNET_GROUP_DIM
MLA_WIDTH = MLA_HEADS * V_HEAD_DIM
XATTN_HEADS = 4
XATTN_HEAD_DIM = D_MODEL // XATTN_HEADS
D_FF = 2816
ROPE_BASE = 10000.0
NORM_EPS = 1e-6
LOG2_E = 1.4426950408889634

LANES = 128
SUBLANES = 8
HEAD_PAD = LANES
FF_CHUNK = 256
VMEM_LIMIT = 56 * 1024 * 1024
TILES = (1024, 512, 512)
ATTN_KEY_CHUNK = 1024
ATTN_PILOT_KEYS = 64
ATTN_SHIFT_MARGIN = 30.0
ATTN_SAFE_GAP = 90.0
ATTN_NORM_SLACK = 1.02
DFT_LEVELS = 3
DFT_ROW_CHUNK = 256

_O_CQ = 0
_O_CKV = _O_CQ + Q_LORA_RANK
_O_KR = _O_CKV + KV_LORA_RANK
_O_U = _O_KR + HEAD_PAD
IN_EXT = _O_U + FNET_WIDTH


def _rms(x, g):
    return x * lax.rsqrt(jnp.mean(x * x, axis=-1, keepdims=True) + NORM_EPS) * g


def _dot(a, b):
    return jnp.dot(a, b, preferred_element_type=F32)


def _dot_nt(a, b):
    return lax.dot_general(a, b, (((1,), (1,)), ((), ())), preferred_element_type=F32)


def _const_spec(shape):
    zeros = (0,) * len(shape)
    return pl.BlockSpec(shape, lambda *_: zeros, pipeline_mode=pl.Buffered(1))


def _params(n_axes):
    return pltpu.CompilerParams(dimension_semantics=("arbitrary",) * n_axes, vmem_limit_bytes=VMEM_LIMIT)


def _proj_kernel(x_ref, ctab_ref, stab_ref, cos_t_ref, sin_t_ref, ln_g_ref, w_in_ref, qn_g_ref, w_q_t_ref,
                 kvn_g_ref, w_ukv_ref, w_uv_t_ref, w_cdft_ref, head_sel_ref,
                 q_t_ref, k_ref, v_t_ref, kn2_ref, qn2_ref, lb_ref, vr_ref, vi_ref):
    scale = (QK_NOPE_DIM + QK_ROPE_DIM) ** -0.5 * LOG2_E
    half = QK_ROPE_DIM // 2
    qk_dim = QK_NOPE_DIM + QK_ROPE_DIM
    h = _rms(x_ref[0], ln_g_ref[...]).astype(BF16)
    z = _dot(h, w_in_ref[...])

    cq = _rms(z[:, _O_CQ:_O_CQ + Q_LORA_RANK], qn_g_ref[...]).astype(BF16)
    q_t = _dot_nt(w_q_t_ref[...], cq)
    cos_t = cos_t_ref[...] * scale
    sin_t = sin_t_ref[...] * scale
    zero_rows = jnp.zeros((HEAD_PAD - qk_dim, q_t.shape[1]), BF16)
    q_heads = []
    for hd in range(MLA_HEADS):
        src, dst = hd * qk_dim, hd * HEAD_PAD
        x1 = q_t[src + QK_NOPE_DIM:src + QK_NOPE_DIM + half, :]
        x2 = q_t[src + QK_NOPE_DIM + half:src + qk_dim, :]
        q_h = jnp.concatenate([q_t[src:src + QK_NOPE_DIM, :] * scale, x1 * cos_t - x2 * sin_t,
                               x2 * cos_t + x1 * sin_t], axis=0)
        q_heads.append(jnp.concatenate([q_h.astype(BF16), zero_rows], axis=0))
        q_t_ref[0, dst:dst + HEAD_PAD, :] = q_heads[hd]
        qn2_ref[0, hd:hd + 1, :] = jnp.sum(q_h * q_h, axis=0, keepdims=True)

    ckv = _rms(z[:, _O_CKV:_O_CKV + KV_LORA_RANK], kvn_g_ref[...]).astype(BF16)
    kv = _dot(ckv, w_ukv_ref[...])
    zk = z[:, _O_KR:_O_KR + HEAD_PAD]
    kr = zk * ctab_ref[...] + pltpu.roll(zk, HEAD_PAD - QK_ROPE_DIM, 1) * stab_ref[...]
    nope = lax.broadcasted_iota(jnp.int32, kr.shape, 1) < QK_NOPE_DIM
    k_heads = [jnp.where(nope, kv[:, hd * HEAD_PAD:(hd + 1) * HEAD_PAD], kr) for hd in range(MLA_HEADS)]
    for hd in range(MLA_HEADS):
        k_ref[0, hd] = k_heads[hd].astype(BF16)
    k_all = jnp.concatenate(k_heads, axis=1)
    kn2 = _dot((k_all * k_all).astype(BF16), head_sel_ref[...])
    kn2_ref[0, 0] = jnp.broadcast_to(jnp.max(kn2, axis=0, keepdims=True), kn2_ref.shape[2:])

    pilot_k = k_all[:ATTN_PILOT_KEYS, :].astype(BF16)
    for hd in range(MLA_HEADS):
        pilot = _dot(pilot_k[:, hd * HEAD_PAD:(hd + 1) * HEAD_PAD], q_heads[hd])
        lb_ref[0, hd:hd + 1, :] = jnp.max(pilot, axis=0, keepdims=True)
    v_t_ref[0] = _dot_nt(w_uv_t_ref[...], ckv).astype(BF16)

    for gi in range(FNET_GROUPS):
        sl = slice(gi * FNET_GROUP_DIM, (gi + 1) * FNET_GROUP_DIM)
        v = _dot(z[:, _O_U + gi * FNET_GROUP_DIM:_O_U + (gi + 1) * FNET_GROUP_DIM].astype(BF16), w_cdft_ref[...])
        vr_ref[0, :, sl] = v[:, :FNET_GROUP_DIM].astype(BF16)
        vi_ref[0, :, sl] = v[:, FNET_GROUP_DIM:].astype(BF16)


def _proj(x, ctab, stab, cos_t, sin_t, ln_g, w_in_ext, qn_g, w_q_t, kvn_g, w_ukv, w_uv_t, w_cdft, tile):
    B, S, _ = x.shape
    hp = MLA_HEADS * HEAD_PAD
    tok = lambda w: pl.BlockSpec((1, tile, w), lambda b, t: (b, t, 0))
    tok_t = lambda w: pl.BlockSpec((1, w, tile), lambda b, t: (b, 0, t))
    tab = pl.BlockSpec((tile, LANES), lambda b, t: (t, 0))
    tab_t = pl.BlockSpec((cos_t.shape[0], tile), lambda b, t: (0, t))
    head_sel = jnp.asarray(np.arange(hp)[:, None] // HEAD_PAD == np.arange(LANES)[None, :], BF16)
    return pl.pallas_call(
        _proj_kernel,
        grid=(B, S // tile),
        in_specs=[tok(D_MODEL), tab, tab, tab_t, tab_t, _const_spec(ln_g.shape), _const_spec(w_in_ext.shape),
                  _const_spec(qn_g.shape), _const_spec(w_q_t.shape), _const_spec(kvn_g.shape),
                  _const_spec(w_ukv.shape), _const_spec(w_uv_t.shape), _const_spec(w_cdft.shape),
                  _const_spec(head_sel.shape)],
        out_specs=[tok_t(hp), pl.BlockSpec((1, MLA_HEADS, tile, HEAD_PAD), lambda b, t: (b, 0, t, 0)),
                   tok_t(MLA_WIDTH), pl.BlockSpec((1, 1, SUBLANES, LANES), lambda b, t: (b, t, 0, 0)),
                   tok_t(MLA_HEADS), tok_t(MLA_HEADS),
                   tok(FNET_WIDTH), tok(FNET_WIDTH)],
        out_shape=[jax.ShapeDtypeStruct((B, hp, S), BF16), jax.ShapeDtypeStruct((B, MLA_HEADS, S, HEAD_PAD), BF16),
                   jax.ShapeDtypeStruct((B, MLA_WIDTH, S), BF16), jax.ShapeDtypeStruct((B, S // tile, SUBLANES, LANES), F32),
                   jax.ShapeDtypeStruct((B, MLA_HEADS, S), F32), jax.ShapeDtypeStruct((B, MLA_HEADS, S), F32)]
                  + [jax.ShapeDtypeStruct((B, S, FNET_WIDTH), BF16)] * 2,
        compiler_params=_params(2),
        name="proj",
    )(x, ctab, stab, cos_t, sin_t, ln_g, w_in_ext, qn_g, w_q_t, kvn_g, w_ukv, w_uv_t, w_cdft, head_sel)


def _attn_kernel(safe_ref, q_t_ref, k_ref, v_t_ref, shift_ref, o_ref, s_scr):
    S = k_ref.shape[2]
    kc = min(ATTN_KEY_CHUNK, S)
    n_chunks = S // kc
    heads = range(MLA_HEADS)

    def k_blk(hh, r0, r1):
        return k_ref[0, hh, r0:r1, :]

    def q_blk(hh):
        return q_t_ref[0, hh * HEAD_PAD:(hh + 1) * HEAD_PAD, :]

    def v_blk(hh, c):
        return v_t_ref[0, hh * V_HEAD_DIM:(hh + 1) * V_HEAD_DIM, c * kc:(c + 1) * kc]

    def finish(acc, den):
        o_ref[0] = jnp.concatenate([a / l for a, l in zip(acc, den)], axis=0).T.astype(BF16)

    all_safe = safe_ref[pl.program_id(0), pl.program_id(1)] != 0

    @pl.when(all_safe)
    def _single_pass():
        items = [(hh, c) for hh in heads for c in range(n_chunks)]
        score = lambda hh, c: _dot(k_blk(hh, c * kc, (c + 1) * kc), q_blk(hh))
        acc, den = [None] * len(heads), [None] * len(heads)
        s_next = score(*items[0])
        for i, (hh, c) in enumerate(items):
            s = s_next
            if i + 1 < len(items):
                s_next = score(*items[i + 1])
            p = jnp.exp2(s - shift_ref[0, hh:hh + 1, :])
            o_c, l_c = _dot(v_blk(hh, c), p.astype(BF16)), jnp.sum(p, axis=0, keepdims=True)
            acc[hh] = o_c if acc[hh] is None else acc[hh] + o_c
            den[hh] = l_c if den[hh] is None else den[hh] + l_c
        finish(acc, den)

    @pl.when(jnp.logical_not(all_safe))
    def _two_pass():
        def score_chunk(hh, c):
            s = _dot(k_blk(hh, c * kc, (c + 1) * kc), q_blk(hh))
            s_scr[hh % 2, c * kc:(c + 1) * kc, :] = s
            return jnp.max(s, axis=0, keepdims=True)

        items = [(hh, c) for hh in heads for c in range(n_chunks)]
        m_parts = [[] for _ in heads]
        m_head, acc, den = [None] * len(heads), [None] * len(heads), [None] * len(heads)
        for i in range(len(items) + n_chunks):
            if i < len(items):
                hh, c = items[i]
                m_parts[hh].append(score_chunk(hh, c))
                if c == n_chunks - 1:
                    m_head[hh] = functools.reduce(jnp.maximum, m_parts[hh])
            if i >= n_chunks:
                hh, c = items[i - n_chunks]
                p = jnp.exp2(s_scr[hh % 2, c * kc:(c + 1) * kc, :] - m_head[hh])
                o_c, l_c = _dot(v_blk(hh, c), p.astype(BF16)), jnp.sum(p, axis=0, keepdims=True)
                acc[hh] = o_c if acc[hh] is None else acc[hh] + o_c
                den[hh] = l_c if den[hh] is None else den[hh] + l_c
        finish(acc, den)


def _score_shifts(kn2, qn2, lb, tile):
    B, _, S = lb.shape
    kmax2 = jnp.max(kn2[:, :, 0, :MLA_HEADS], axis=1)
    ub = jnp.sqrt(qn2 * kmax2[:, :, None]) * ATTN_NORM_SLACK
    ok = (ub - lb) <= ATTN_SAFE_GAP
    safe = jnp.all(ok.reshape(B, MLA_HEADS, S // tile, tile), axis=(1, 3))
    return lb + ATTN_SHIFT_MARGIN, safe.astype(jnp.int32)


def _attn(q_t, k, v_t, shift, safe, tile):
    B, _, S, _ = k.shape
    hp = MLA_HEADS * HEAD_PAD
    return pl.pallas_call(
        _attn_kernel,
        grid_spec=pltpu.PrefetchScalarGridSpec(
            num_scalar_prefetch=1,
            grid=(B, S // tile),
            in_specs=[pl.BlockSpec((1, hp, tile), lambda b, t, safe: (b, 0, t)),
                      pl.BlockSpec((1, MLA_HEADS, S, HEAD_PAD), lambda b, t, safe: (b, 0, 0, 0)),
                      pl.BlockSpec((1, MLA_WIDTH, S), lambda b, t, safe: (b, 0, 0)),
                      pl.BlockSpec((1, MLA_HEADS, tile), lambda b, t, safe: (b, 0, t))],
            out_specs=pl.BlockSpec((1, tile, MLA_WIDTH), lambda b, t, safe: (b, t, 0)),
            scratch_shapes=[pltpu.VMEM((2, S, tile), F32)]),
        out_shape=jax.ShapeDtypeStruct((B, S, MLA_WIDTH), BF16),
        compiler_params=_params(2),
        name="attn",
    )(safe, q_t, k, v_t, shift)


def _bit_reverse(i, bits):
    return int(format(i, "0%db" % bits)[::-1], 2)


def _dft_kernel(vr_ref, vi_ref, twc_ref, tws_ref, cs_ref, w_f_ref, g_ref, xr_scr, xi_scr, xb_scr):
    S = vr_ref.shape[1]
    n_leaves = 2 ** DFT_LEVELS
    leaf = S // n_leaves
    chunk = min(DFT_ROW_CHUNK, leaf)
    reps = FNET_WIDTH // LANES
    n, off = S, 0
    for lev in range(DFT_LEVELS):
        half = n // 2
        last = lev == DFT_LEVELS - 1
        for base in range(0, S, n):
            for c0 in range(0, half, chunk):
                top = slice(base + c0, base + c0 + chunk)
                bot = slice(base + half + c0, base + half + c0 + chunk)
                if lev == 0:
                    tr, ti = vr_ref[0, top, :].astype(F32), vi_ref[0, top, :].astype(F32)
                    br, bi = vr_ref[0, bot, :].astype(F32), vi_ref[0, bot, :].astype(F32)
                else:
                    tr, ti, br, bi = xr_scr[top, :], xi_scr[top, :], xr_scr[bot, :], xi_scr[bot, :]
                c = jnp.concatenate([twc_ref[off + c0:off + c0 + chunk, :]] * reps, axis=1)
                s = jnp.concatenate([tws_ref[off + c0:off + c0 + chunk, :]] * reps, axis=1)
                dr, di = tr - br, ti - bi
                ar, ai = tr + br, ti + bi
                mr, mi = dr * c + di * s, di * c - dr * s
                if last:
                    blk, r0 = (base // n) * 2, c0
                    xb_scr[blk, r0:r0 + chunk, :] = ar.astype(BF16)
                    xb_scr[blk, leaf + r0:leaf + r0 + chunk, :] = ai.astype(BF16)
                    xb_scr[blk + 1, r0:r0 + chunk, :] = mr.astype(BF16)
                    xb_scr[blk + 1, leaf + r0:leaf + r0 + chunk, :] = mi.astype(BF16)
                else:
                    xr_scr[top, :], xi_scr[top, :] = ar, ai
                    xr_scr[bot, :], xi_scr[bot, :] = mr, mi
        off += half
        n = half
    for blk in range(n_leaves):
        f = _dot(cs_ref[...], xb_scr[blk])
        g = _dot(f.astype(BF16), w_f_ref[...])
        g_ref[0, _bit_reverse(blk, DFT_LEVELS)] = g.astype(BF16)


def _dft(vr, vi, twc, tws, cs, w_f):
    B, S, _ = vr.shape
    n_leaves = 2 ** DFT_LEVELS
    leaf = S // n_leaves
    bat = pl.BlockSpec((1, S, FNET_WIDTH), lambda b: (b, 0, 0))
    return pl.pallas_call(
        _dft_kernel,
        grid=(B,),
        in_specs=[bat, bat, _const_spec(twc.shape), _const_spec(tws.shape), _const_spec(cs.shape),
                  _const_spec(w_f.shape)],
        out_specs=pl.BlockSpec((1, n_leaves, leaf, FNET_WIDTH), lambda b: (b, 0, 0, 0)),
        out_shape=jax.ShapeDtypeStruct((B, n_leaves, leaf, FNET_WIDTH), BF16),
        scratch_shapes=[pltpu.VMEM((S, FNET_WIDTH), F32), pltpu.VMEM((S, FNET_WIDTH), F32),
                        pltpu.VMEM((n_leaves, 2 * leaf, FNET_WIDTH), BF16)],
        compiler_params=_params(1),
        name="dft",
    )(vr, vi, twc, tws, cs, w_f)


def _post_kernel(x_ref, o_ref, g_ref, mem_ref, w_out_ref, ln_x_g_ref, mem_g_ref, w_xq_ref, w_xkv_ref,
                 w_xo_ref, ln_f_g_ref, w_gu_ref, w_d_ref, fin_g_ref, y_ref, kv_scr, a_scr, g_scr):
    @pl.when(pl.program_id(1) == 0)
    def _():
        m = _rms(mem_ref[0], mem_g_ref[...]).astype(BF16)
        kv_scr[...] = _dot(m, w_xkv_ref[...]).astype(BF16)

    n_res, rows_per_res = g_ref.shape[1], g_ref.shape[2]
    for r in range(n_res):
        g_r = g_ref[0, r].astype(F32)
        for lt in range(FNET_WIDTH // LANES):
            g_scr[lt, pl.ds(r, rows_per_res, stride=n_res), :] = g_r[:, lt * LANES:(lt + 1) * LANES]
    g = jnp.concatenate([g_scr[lt] for lt in range(FNET_WIDTH // LANES)], axis=1).astype(BF16)

    x1 = (x_ref[0] + _dot(o_ref[0], w_out_ref[:MLA_WIDTH, :])
          + _dot(g, w_out_ref[MLA_WIDTH:, :]))

    hq = _rms(x1, ln_x_g_ref[...]).astype(BF16)
    q = (_dot(hq, w_xq_ref[...]) * (XATTN_HEAD_DIM ** -0.5 * LOG2_E)).astype(BF16)
    heads = []
    for hd in range(XATTN_HEADS):
        sl = slice(hd * XATTN_HEAD_DIM, (hd + 1) * XATTN_HEAD_DIM)
        s = _dot_nt(q[:, sl], kv_scr[:, sl])
        m = jnp.max(s, axis=-1, keepdims=True)
        p = jnp.exp2(s - m)
        l = jnp.sum(p, axis=-1, keepdims=True)
        vh = kv_scr[:, D_MODEL + hd * XATTN_HEAD_DIM:D_MODEL + (hd + 1) * XATTN_HEAD_DIM]
        heads.append((_dot(p.astype(BF16), vh) / l).astype(BF16))
    x2 = x1 + _dot(jnp.concatenate(heads, axis=-1), w_xo_ref[...])

    hf = _rms(x2, ln_f_g_ref[...]).astype(BF16)
    for c in range(D_FF // FF_CHUNK):
        sl = slice(c * FF_CHUNK, (c + 1) * FF_CHUNK)
        gate = _dot(hf, w_gu_ref[:, sl])
        up = _dot(hf, w_gu_ref[:, D_FF + c * FF_CHUNK:D_FF + (c + 1) * FF_CHUNK])
        a_scr[:, sl] = (gate * jax.nn.sigmoid(gate) * up).astype(BF16)
    x3 = x2 + _dot(a_scr[...], w_d_ref[...])
    y_ref[0] = _rms(x3, fin_g_ref[...])


def _post(x, o, g, mem, w_out, ln_x_g, mem_g, w_xq, w_xkv, w_xo, ln_f_g, w_gu, w_d, fin_g, tile):
    B, S, _ = x.shape
    M = mem.shape[1]
    n_res = g.shape[1]
    tok = lambda w: pl.BlockSpec((1, tile, w), lambda b, t: (b, t, 0))
    consts = [w_out, ln_x_g, mem_g, w_xq, w_xkv, w_xo, ln_f_g, w_gu, w_d, fin_g]
    return pl.pallas_call(
        _post_kernel,
        grid=(B, S // tile),
        in_specs=[tok(D_MODEL), tok(MLA_WIDTH),
                  pl.BlockSpec((1, n_res, tile // n_res, FNET_WIDTH), lambda b, t: (b, 0, t, 0)),
                  pl.BlockSpec((1, M, D_MODEL), lambda b, t: (b, 0, 0))]
                 + [_const_spec(c.shape) for c in consts],
        out_specs=tok(D_MODEL),
        out_shape=jax.ShapeDtypeStruct((B, S, D_MODEL), F32),
        scratch_shapes=[pltpu.VMEM((M, 2 * D_MODEL), BF16), pltpu.VMEM((tile, D_FF), BF16),
                        pltpu.VMEM((FNET_WIDTH // LANES, tile, LANES), F32)],
        compiler_params=_params(2),
        name="post",
    )(x, o, g, mem, *consts)


def _rot_cols(w):
    half = QK_ROPE_DIM // 2
    return jnp.concatenate([-w[..., half:], w[..., :half]], axis=-1)


def _prep_weights(w_in, w_uq, w_ukv, w_fnet):
    w_cq = w_in[:, :Q_LORA_RANK]
    w_ckv = w_in[:, Q_LORA_RANK:Q_LORA_RANK + KV_LORA_RANK]
    o3 = Q_LORA_RANK + KV_LORA_RANK
    w_kr = w_in[:, o3:o3 + QK_ROPE_DIM]
    w_u = w_in[:, o3 + QK_ROPE_DIM:]
    w_kr_grp = jnp.concatenate([jnp.zeros((D_MODEL, QK_NOPE_DIM), F32), w_kr, _rot_cols(w_kr)], axis=-1)
    w_in_ext = jnp.concatenate([w_cq, w_ckv, w_kr_grp, w_u], axis=-1).astype(BF16)

    w_q_t = w_uq.T.astype(BF16)

    kv_w = w_ukv.reshape(KV_LORA_RANK, MLA_HEADS, QK_NOPE_DIM + V_HEAD_DIM)
    w_uv_t = kv_w[..., QK_NOPE_DIM:].reshape(KV_LORA_RANK, MLA_WIDTH).T.astype(BF16)

    w_f = jnp.zeros((FNET_WIDTH, FNET_WIDTH), F32)
    for gi in range(FNET_GROUPS):
        sl = slice(gi * FNET_GROUP_DIM, (gi + 1) * FNET_GROUP_DIM)
        w_f = w_f.at[sl, sl].set(w_fnet[gi])
    return w_in_ext, w_q_t, w_uv_t, w_f.astype(BF16)


def _dft_mats(n):
    idx = np.arange(n, dtype=np.int64)
    ang = ((idx[:, None] * idx[None, :]) % n) * (2.0 * np.pi / n)
    return np.cos(ang), np.sin(ang)


def _seq_dft_tables(seq):
    cs_rows, sn_rows = [], []
    n = seq
    for _ in range(DFT_LEVELS):
        ang = np.arange(n // 2) * (2.0 * np.pi / n)
        cs_rows.append(np.cos(ang))
        sn_rows.append(np.sin(ang))
        n //= 2
    rep = lambda rows: jnp.asarray(np.broadcast_to(np.concatenate(rows)[:, None], (seq - n, LANES)), F32)
    c, s = _dft_mats(n)
    leaf = jnp.asarray(np.concatenate([c, s], axis=1) * seq ** -0.5, F32).astype(BF16)
    return rep(cs_rows), rep(sn_rows), leaf


def _channel_dft():
    c, s = _dft_mats(FNET_GROUP_DIM)
    return jnp.asarray(np.concatenate([c, -s], axis=-1) * FNET_GROUP_DIM ** -0.5, F32).astype(BF16)


def _rope_tabs(seq):
    inv = 1.0 / (ROPE_BASE ** (np.arange(0, QK_ROPE_DIM, 2) / QK_ROPE_DIM))
    ang = np.arange(seq)[:, None] * inv[None, :]
    cos, sin = np.cos(ang), np.sin(ang)
    pad = lambda t: jnp.asarray(np.concatenate(
        [np.zeros((seq, QK_NOPE_DIM)), t, t, np.zeros((seq, HEAD_PAD - QK_NOPE_DIM - QK_ROPE_DIM))], axis=-1), F32)
    return pad(cos), pad(sin), jnp.asarray(cos.T, F32), jnp.asarray(sin.T, F32)


def _trunk(x, mem, w, tiles):
    S = x.shape[1]
    ctab, stab, cos_t, sin_t = _rope_tabs(S)
    twc, tws, cs = _seq_dft_tables(S)
    q_t, k, v_t, kn2, qn2, lb, vr, vi = _proj(x, ctab, stab, cos_t, sin_t, w["ln_mix_g"], w["w_in_ext"],
                                              w["q_norm_g"], w["w_q_t"], w["kv_norm_g"], w["w_ukv"],
                                              w["w_uv_t"], w["w_cdft"], min(tiles[0], S))
    shift, safe = _score_shifts(kn2, qn2, lb, min(tiles[1], S))
    o = _attn(q_t, k, v_t, shift, safe, min(tiles[1], S))
    g = _dft(vr, vi, twc, tws, cs, w["w_f"])
    return _post(x, o, g, mem, w["w_out"], w["ln_x_g"], w["mem_norm_g"], w["w_xq"], w["w_xkv"], w["w_xo"],
                 w["ln_ffn_g"], w["w_gate_up"], w["w_down"], w["final_norm_g"], min(tiles[2], S))


def kernel(x_prompt, x_sample, mem_prompt, mem_sample, ln_mix_g, w_in, q_norm_g, w_uq, kv_norm_g, w_ukv, w_fnet,
           w_out, ln_x_g, mem_norm_g, w_xq, w_xkv, w_xo, ln_ffn_g, w_gate_up, w_down, final_norm_g):
    assert ln_mix_g.shape[0] == 1, "single-layer trunk"
    w_in_ext, w_q_t, w_uv_t, w_f = _prep_weights(w_in[0], w_uq[0], w_ukv[0], w_fnet[0])
    row = lambda g: g.reshape(1, -1).astype(F32)
    w = dict(
        ln_mix_g=row(ln_mix_g[0]), w_in_ext=w_in_ext, q_norm_g=row(q_norm_g[0]), w_q_t=w_q_t,
        kv_norm_g=row(kv_norm_g[0]), w_ukv=w_ukv[0].astype(BF16), w_uv_t=w_uv_t, w_cdft=_channel_dft(), w_f=w_f,
        w_out=w_out[0].astype(BF16), ln_x_g=row(ln_x_g[0]), mem_norm_g=row(mem_norm_g[0]),
        w_xq=w_xq[0].astype(BF16), w_xkv=w_xkv[0].astype(BF16), w_xo=w_xo[0].astype(BF16),
        ln_ffn_g=row(ln_ffn_g[0]), w_gate_up=w_gate_up[0].astype(BF16), w_down=w_down[0].astype(BF16),
        final_norm_g=row(final_norm_g),
    )
    return (_trunk(x_prompt, mem_prompt, w, TILES), _trunk(x_sample, mem_sample, w, TILES))
```

```python
import functools

import numpy as np
import jax
import jax.numpy as jnp
from jax import lax
from jax.experimental import pallas as pl
from jax.experimental.pallas import tpu as pltpu

F32 = jnp.float32
BF16 = jnp.bfloat16

D_MODEL = 1024
MLA_HEADS = 8
QK_NOPE_DIM = 64
QK_ROPE_DIM = 32
V_HEAD_DIM = 64
Q_LORA_RANK = 384
KV_LORA_RANK = 256
FNET_GROUPS = 4
FNET_GROUP_DIM = 128
FNET_WIDTH = FNET_GROUPS * FNET_GROUP_DIM
MLA_WIDTH = MLA_HEADS * V_HEAD_DIM
XATTN_HEADS = 4
XATTN_HEAD_DIM = D_MODEL // XATTN_HEADS
D_FF = 2816
ROPE_BASE = 10000.0
NORM_EPS = 1e-6
LOG2_E = 1.4426950408889634

LANES = 128
SUBLANES = 8
HEAD_PAD = LANES
FF_CHUNK = 256
VMEM_LIMIT = 56 * 1024 * 1024
TILES = (1024, 512, 512)
ATTN_KEY_CHUNK = 1024
ATTN_PILOT_KEYS = 64
ATTN_SHIFT_MARGIN = 30.0
ATTN_SAFE_GAP = 90.0
ATTN_NORM_SLACK = 1.02
DFT_LEVELS = 3
DFT_ROW_CHUNK = 256

_O_CQ = 0
_O_CKV = _O_CQ + Q_LORA_RANK
_O_KR = _O_CKV + KV_LORA_RANK
_O_U = _O_KR + HEAD_PAD
IN_EXT = _O_U + FNET_WIDTH


def _rms(x, g):
    return x * lax.rsqrt(jnp.mean(x * x, axis=-1, keepdims=True) + NORM_EPS) * g


def _dot(a, b):
    return jnp.dot(a, b, preferred_element_type=F32)


def _dot_nt(a, b):
    return lax.dot_general(a, b, (((1,), (1,)), ((), ())), preferred_element_type=F32)


def _const_spec(shape):
    zeros = (0,) * len(shape)
    return pl.BlockSpec(shape, lambda *_: zeros, pipeline_mode=pl.Buffered(1))


def _params(n_axes):
    return pltpu.CompilerParams(dimension_semantics=("arbitrary",) * n_axes, vmem_limit_bytes=VMEM_LIMIT)


def _proj_kernel(x_ref, ctab_ref, stab_ref, cos_t_ref, sin_t_ref, ln_g_ref, w_in_ref, qn_g_ref, w_q_t_ref,
                 kvn_g_ref, w_ukv_ref, w_uv_t_ref, w_cdft_ref,
                 q_t_ref, k_ref, v_t_ref, kn2_ref, qn2_ref, lb_ref, vr_ref, vi_ref):
    scale = (QK_NOPE_DIM + QK_ROPE_DIM) ** -0.5 * LOG2_E
    half = QK_ROPE_DIM // 2
    qk_dim = QK_NOPE_DIM + QK_ROPE_DIM
    h = _rms(x_ref[0], ln_g_ref[...]).astype(BF16)
    z = _dot(h, w_in_ref[...])

    cq = _rms(z[:, _O_CQ:_O_CQ + Q_LORA_RANK], qn_g_ref[...]).astype(BF16)
    q_t = _dot_nt(w_q_t_ref[...], cq)
    cos_t = cos_t_ref[...] * scale
    sin_t = sin_t_ref[...] * scale
    zero_rows = jnp.zeros((HEAD_PAD - qk_dim, q_t.shape[1]), BF16)
    q_heads = []
    for hd in range(MLA_HEADS):
        src, dst = hd * qk_dim, hd * HEAD_PAD
        x1 = q_t[src + QK_NOPE_DIM:src + QK_NOPE_DIM + half, :]
        x2 = q_t[src + QK_NOPE_DIM + half:src + qk_dim, :]
        q_h = jnp.concatenate([q_t[src:src + QK_NOPE_DIM, :] * scale, x1 * cos_t - x2 * sin_t,
                               x2 * cos_t + x1 * sin_t], axis=0)
        q_heads.append(jnp.concatenate([q_h.astype(BF16), zero_rows], axis=0))
        q_t_ref[0, dst:dst + HEAD_PAD, :] = q_heads[hd]
        qn2_ref[0, hd:hd + 1, :] = jnp.sum(q_h * q_h, axis=0, keepdims=True)

    ckv = _rms(z[:, _O_CKV:_O_CKV + KV_LORA_RANK], kvn_g_ref[...]).astype(BF16)
    kv = _dot(ckv, w_ukv_ref[...])
    zk = z[:, _O_KR:_O_KR + HEAD_PAD]
    kr = zk * ctab_ref[...] + pltpu.roll(zk, HEAD_PAD - QK_ROPE_DIM, 1) * stab_ref[...]
    nope = lax.broadcasted_iota(jnp.int32, kr.shape, 1) < QK_NOPE_DIM
    k_heads = [jnp.where(nope, kv[:, hd * HEAD_PAD:(hd + 1) * HEAD_PAD], kr) for hd in range(MLA_HEADS)]
    for hd in range(MLA_HEADS):
        k_ref[0, hd] = k_heads[hd].astype(BF16)
    lane1 = lax.broadcasted_iota(jnp.int32, (1, LANES), 1)
    kn2 = jnp.zeros((1, LANES), F32)
    for hd in range(MLA_HEADS):
        row_n2 = jnp.sum(k_heads[hd] * k_heads[hd], axis=1, keepdims=True)
        kn2 = jnp.where(lane1 == hd, jnp.max(row_n2, axis=0, keepdims=True), kn2)
        pilot = _dot(k_heads[hd][:ATTN_PILOT_KEYS, :].astype(BF16), q_heads[hd])
        lb_ref[0, hd:hd + 1, :] = jnp.max(pilot, axis=0, keepdims=True)
    kn2_ref[0, 0] = jnp.broadcast_to(kn2, kn2_ref.shape[2:])
    v_t_ref[0] = _dot_nt(w_uv_t_ref[...], ckv).astype(BF16)

    for gi in range(FNET_GROUPS):
        sl = slice(gi * FNET_GROUP_DIM, (gi + 1) * FNET_GROUP_DIM)
        v = _dot(z[:, _O_U + gi * FNET_GROUP_DIM:_O_U + (gi + 1) * FNET_GROUP_DIM].astype(BF16), w_cdft_ref[...])
        vr_ref[0, :, sl] = v[:, :FNET_GROUP_DIM].astype(BF16)
        vi_ref[0, :, sl] = v[:, FNET_GROUP_DIM:].astype(BF16)


def _proj(x, ctab, stab, cos_t, sin_t, ln_g, w_in_ext, qn_g, w_q_t, kvn_g, w_ukv, w_uv_t, w_cdft, tile):
    B, S, _ = x.shape
    hp = MLA_HEADS * HEAD_PAD
    tok = lambda w: pl.BlockSpec((1, tile, w), lambda b, t: (b, t, 0))
    tok_t = lambda w: pl.BlockSpec((1, w, tile), lambda b, t: (b, 0, t))
    tab = pl.BlockSpec((tile, LANES), lambda b, t: (t, 0))
    tab_t = pl.BlockSpec((cos_t.shape[0], tile), lambda b, t: (0, t))
    return pl.pallas_call(
        _proj_kernel,
        grid=(B, S // tile),
        in_specs=[tok(D_MODEL), tab, tab, tab_t, tab_t, _const_spec(ln_g.shape), _const_spec(w_in_ext.shape),
                  _const_spec(qn_g.shape), _const_spec(w_q_t.shape), _const_spec(kvn_g.shape),
                  _const_spec(w_ukv.shape), _const_spec(w_uv_t.shape), _const_spec(w_cdft.shape)],
        out_specs=[tok_t(hp), pl.BlockSpec((1, MLA_HEADS, tile, HEAD_PAD), lambda b, t: (b, 0, t, 0)),
                   tok_t(MLA_WIDTH), pl.BlockSpec((1, 1, SUBLANES, LANES), lambda b, t: (b, t, 0, 0)),
                   tok_t(MLA_HEADS), tok_t(MLA_HEADS),
                   tok(FNET_WIDTH), tok(FNET_WIDTH)],
        out_shape=[jax.ShapeDtypeStruct((B, hp, S), BF16), jax.ShapeDtypeStruct((B, MLA_HEADS, S, HEAD_PAD), BF16),
                   jax.ShapeDtypeStruct((B, MLA_WIDTH, S), BF16), jax.ShapeDtypeStruct((B, S // tile, SUBLANES, LANES), F32),
                   jax.ShapeDtypeStruct((B, MLA_HEADS, S), F32), jax.ShapeDtypeStruct((B, MLA_HEADS, S), F32)]
                  + [jax.ShapeDtypeStruct((B, S, FNET_WIDTH), BF16)] * 2,
        compiler_params=_params(2),
        name="proj",
    )(x, ctab, stab, cos_t, sin_t, ln_g, w_in_ext, qn_g, w_q_t, kvn_g, w_ukv, w_uv_t, w_cdft)


def _attn_kernel(safe_ref, q_t_ref, k_ref, v_t_ref, shift_ref, o_ref, s_scr):
    S = k_ref.shape[2]
    kc = min(ATTN_KEY_CHUNK, S)
    n_chunks = S // kc
    heads = range(MLA_HEADS)

    def k_blk(hh, r0, r1):
        return k_ref[0, hh, r0:r1, :]

    def q_blk(hh):
        return q_t_ref[0, hh * HEAD_PAD:(hh + 1) * HEAD_PAD, :]

    def v_blk(hh, c):
        return v_t_ref[0, hh * V_HEAD_DIM:(hh + 1) * V_HEAD_DIM, c * kc:(c + 1) * kc]

    def finish(acc, den):
        o_ref[0] = jnp.concatenate([a / l for a, l in zip(acc, den)], axis=0).T.astype(BF16)

    all_safe = safe_ref[pl.program_id(0), pl.program_id(1)] != 0

    @pl.when(all_safe)
    def _single_pass():
        items = [(hh, c) for hh in heads for c in range(n_chunks)]
        score = lambda hh, c: _dot(k_blk(hh, c * kc, (c + 1) * kc), q_blk(hh))
        acc, den = [None] * len(heads), [None] * len(heads)
        s_next = score(*items[0])
        for i, (hh, c) in enumerate(items):
            s = s_next
            if i + 1 < len(items):
                s_next = score(*items[i + 1])
            p = jnp.exp2(s - shift_ref[0, hh:hh + 1, :])
            o_c, l_c = _dot(v_blk(hh, c), p.astype(BF16)), jnp.sum(p, axis=0, keepdims=True)
            acc[hh] = o_c if acc[hh] is None else acc[hh] + o_c
            den[hh] = l_c if den[hh] is None else den[hh] + l_c
        finish(acc, den)

    @pl.when(jnp.logical_not(all_safe))
    def _two_pass():
        def score_chunk(hh, c):
            s = _dot(k_blk(hh, c * kc, (c + 1) * kc), q_blk(hh))
            s_scr[hh % 2, c * kc:(c + 1) * kc, :] = s
            return jnp.max(s, axis=0, keepdims=True)

        items = [(hh, c) for hh in heads for c in range(n_chunks)]
        m_parts = [[] for _ in heads]
        m_head, acc, den = [None] * len(heads), [None] * len(heads), [None] * len(heads)
        for i in range(len(items) + n_chunks):
            if i < len(items):
                hh, c = items[i]
                m_parts[hh].append(score_chunk(hh, c))
                if c == n_chunks - 1:
                    m_head[hh] = functools.reduce(jnp.maximum, m_parts[hh])
            if i >= n_chunks:
                hh, c = items[i - n_chunks]
                p = jnp.exp2(s_scr[hh % 2, c * kc:(c + 1) * kc, :] - m_head[hh])
                o_c, l_c = _dot(v_blk(hh, c), p.astype(BF16)), jnp.sum(p, axis=0, keepdims=True)
                acc[hh] = o_c if acc[hh] is None else acc[hh] + o_c
                den[hh] = l_c if den[hh] is None else den[hh] + l_c
        finish(acc, den)


def _score_shifts(kn2, qn2, lb, tile):
    B, _, S = lb.shape
    kmax2 = jnp.max(kn2[:, :, 0, :MLA_HEADS], axis=1)
    ub = jnp.sqrt(qn2 * kmax2[:, :, None]) * ATTN_NORM_SLACK
    ok = (ub - lb) <= ATTN_SAFE_GAP
    safe = jnp.all(ok.reshape(B, MLA_HEADS, S // tile, tile), axis=(1, 3))
    return lb + ATTN_SHIFT_MARGIN, safe.astype(jnp.int32)


def _attn(q_t, k, v_t, shift, safe, tile):
    B, _, S, _ = k.shape
    hp = MLA_HEADS * HEAD_PAD
    return pl.pallas_call(
        _attn_kernel,
        grid_spec=pltpu.PrefetchScalarGridSpec(
            num_scalar_prefetch=1,
            grid=(B, S // tile),
            in_specs=[pl.BlockSpec((1, hp, tile), lambda b, t, safe: (b, 0, t)),
                      pl.BlockSpec((1, MLA_HEADS, S, HEAD_PAD), lambda b, t, safe: (b, 0, 0, 0)),
                      pl.BlockSpec((1, MLA_WIDTH, S), lambda b, t, safe: (b, 0, 0)),
                      pl.BlockSpec((1, MLA_HEADS, tile), lambda b, t, safe: (b, 0, t))],
            out_specs=pl.BlockSpec((1, tile, MLA_WIDTH), lambda b, t, safe: (b, t, 0)),
            scratch_shapes=[pltpu.VMEM((2, S, tile), F32)]),
        out_shape=jax.ShapeDtypeStruct((B, S, MLA_WIDTH), BF16),
        compiler_params=_params(2),
        name="attn",
    )(safe, q_t, k, v_t, shift)


def _bit_reverse(i, bits):
    return int(format(i, "0%db" % bits)[::-1], 2)


def _dft_kernel(vr_ref, vi_ref, twc_ref, tws_ref, cs_ref, w_f_ref, g_ref, xr_scr, xi_scr, xb_scr):
    S = vr_ref.shape[1]
    n_leaves = 2 ** DFT_LEVELS
    leaf = S // n_leaves
    chunk = min(DFT_ROW_CHUNK, leaf)
    reps = FNET_WIDTH // LANES
    n, off = S, 0
    for lev in range(DFT_LEVELS):
        half = n // 2
        last = lev == DFT_LEVELS - 1
        for base in range(0, S, n):
            for c0 in range(0, half, chunk):
                top = slice(base + c0, base + c0 + chunk)
                bot = slice(base + half + c0, base + half + c0 + chunk)
                if lev == 0:
                    tr, ti = vr_ref[0, top, :].astype(F32), vi_ref[0, top, :].astype(F32)
                    br, bi = vr_ref[0, bot, :].astype(F32), vi_ref[0, bot, :].astype(F32)
                else:
                    tr, ti, br, bi = xr_scr[top, :], xi_scr[top, :], xr_scr[bot, :], xi_scr[bot, :]
                c = jnp.concatenate([twc_ref[off + c0:off + c0 + chunk, :]] * reps, axis=1)
                s = jnp.concatenate([tws_ref[off + c0:off + c0 + chunk, :]] * reps, axis=1)
                dr, di = tr - br, ti - bi
                ar, ai = tr + br, ti + bi
                mr, mi = dr * c + di * s, di * c - dr * s
                if last:
                    blk, r0 = (base // n) * 2, c0
                    xb_scr[blk, r0:r0 + chunk, :] = ar.astype(BF16)
                    xb_scr[blk, leaf + r0:leaf + r0 + chunk, :] = ai.astype(BF16)
                    xb_scr[blk + 1, r0:r0 + chunk, :] = mr.astype(BF16)
                    xb_scr[blk + 1, leaf + r0:leaf + r0 + chunk, :] = mi.astype(BF16)
                else:
                    xr_scr[top, :], xi_scr[top, :] = ar, ai
                    xr_scr[bot, :], xi_scr[bot, :] = mr, mi
        off += half
        n = half
    for blk in range(n_leaves):
        f = _dot(cs_ref[...], xb_scr[blk])
        g = _dot(f.astype(BF16), w_f_ref[...])
        g_ref[0, _bit_reverse(blk, DFT_LEVELS)] = g.astype(BF16)


def _dft(vr, vi, twc, tws, cs, w_f):
    B, S, _ = vr.shape
    n_leaves = 2 ** DFT_LEVELS
    leaf = S // n_leaves
    bat = pl.BlockSpec((1, S, FNET_WIDTH), lambda b: (b, 0, 0))
    return pl.pallas_call(
        _dft_kernel,
        grid=(B,),
        in_specs=[bat, bat, _const_spec(twc.shape), _const_spec(tws.shape), _const_spec(cs.shape),
                  _const_spec(w_f.shape)],
        out_specs=pl.BlockSpec((1, n_leaves, leaf, FNET_WIDTH), lambda b: (b, 0, 0, 0)),
        out_shape=jax.ShapeDtypeStruct((B, n_leaves, leaf, FNET_WIDTH), BF16),
        scratch_shapes=[pltpu.VMEM((S, FNET_WIDTH), F32), pltpu.VMEM((S, FNET_WIDTH), F32),
                        pltpu.VMEM((n_leaves, 2 * leaf, FNET_WIDTH), BF16)],
        compiler_params=_params(1),
        name="dft",
    )(vr, vi, twc, tws, cs, w_f)


def _post_kernel(x_ref, o_ref, g_ref, mem_ref, w_out_ref, ln_x_g_ref, mem_g_ref, w_xq_ref, w_xkv_ref,
                 w_xo_ref, ln_f_g_ref, w_gu_ref, w_d_ref, fin_g_ref, y_ref, kv_scr, a_scr, g_scr):
    @pl.when(pl.program_id(1) == 0)
    def _():
        m = _rms(mem_ref[0], mem_g_ref[...]).astype(BF16)
        kv_scr[...] = _dot(m, w_xkv_ref[...]).astype(BF16)

    n_res, rows_per_res = g_ref.shape[1], g_ref.shape[2]
    for r in range(n_res):
        g_r = g_ref[0, r].astype(F32)
        for lt in range(FNET_WIDTH // LANES):
            g_scr[lt, pl.ds(r, rows_per_res, stride=n_res), :] = g_r[:, lt * LANES:(lt + 1) * LANES]
    g = jnp.concatenate([g_scr[lt] for lt in range(FNET_WIDTH // LANES)], axis=1).astype(BF16)

    x1 = x_ref[0] + _dot(jnp.concatenate([o_ref[0], g], axis=1), w_out_ref[...])

    hq = _rms(x1, ln_x_g_ref[...]).astype(BF16)
    q = (_dot(hq, w_xq_ref[...]) * (XATTN_HEAD_DIM ** -0.5 * LOG2_E)).astype(BF16)
    heads = []
    for hd in range(XATTN_HEADS):
        sl = slice(hd * XATTN_HEAD_DIM, (hd + 1) * XATTN_HEAD_DIM)
        s = _dot_nt(q[:, sl], kv_scr[:, sl])
        m = jnp.max(s, axis=-1, keepdims=True)
        p = jnp.exp2(s - m)
        l = jnp.sum(p, axis=-1, keepdims=True)
        vh = kv_scr[:, D_MODEL + hd * XATTN_HEAD_DIM:D_MODEL + (hd + 1) * XATTN_HEAD_DIM]
        heads.append((_dot(p.astype(BF16), vh) / l).astype(BF16))
    x2 = x1 + _dot(jnp.concatenate(heads, axis=-1), w_xo_ref[...])

    hf = _rms(x2, ln_f_g_ref[...]).astype(BF16)
    for c in range(D_FF // FF_CHUNK):
        sl = slice(c * FF_CHUNK, (c + 1) * FF_CHUNK)
        gate = _dot(hf, w_gu_ref[:, sl])
        up = _dot(hf, w_gu_ref[:, D_FF + c * FF_CHUNK:D_FF + (c + 1) * FF_CHUNK])
        a_scr[:, sl] = (gate * jax.nn.sigmoid(gate) * up).astype(BF16)
    x3 = x2 + _dot(a_scr[...], w_d_ref[...])
    y_ref[0] = _rms(x3, fin_g_ref[...])


def _post(x, o, g, mem, w_out, ln_x_g, mem_g, w_xq, w_xkv, w_xo, ln_f_g, w_gu, w_d, fin_g, tile):
    B, S, _ = x.shape
    M = mem.shape[1]
    n_res = g.shape[1]
    tok = lambda w: pl.BlockSpec((1, tile, w), lambda b, t: (b, t, 0))
    consts = [w_out, ln_x_g, mem_g, w_xq, w_xkv, w_xo, ln_f_g, w_gu, w_d, fin_g]
    return pl.pallas_call(
        _post_kernel,
        grid=(B, S // tile),
        in_specs=[tok(D_MODEL), tok(MLA_WIDTH),
                  pl.BlockSpec((1, n_res, tile // n_res, FNET_WIDTH), lambda b, t: (b, 0, t, 0)),
                  pl.BlockSpec((1, M, D_MODEL), lambda b, t: (b, 0, 0))]
                 + [_const_spec(c.shape) for c in consts],
        out_specs=tok(D_MODEL),
        out_shape=jax.ShapeDtypeStruct((B, S, D_MODEL), F32),
        scratch_shapes=[pltpu.VMEM((M, 2 * D_MODEL), BF16), pltpu.VMEM((tile, D_FF), BF16),
                        pltpu.VMEM((FNET_WIDTH // LANES, tile, LANES), F32)],
        compiler_params=_params(2),
        name="post",
    )(x, o, g, mem, *consts)


def _rot_cols(w):
    half = QK_ROPE_DIM // 2
    return jnp.concatenate([-w[..., half:], w[..., :half]], axis=-1)


def _prep_weights(w_in, w_uq, w_ukv, w_fnet):
    w_cq = w_in[:, :Q_LORA_RANK]
    w_ckv = w_in[:, Q_LORA_RANK:Q_LORA_RANK + KV_LORA_RANK]
    o3 = Q_LORA_RANK + KV_LORA_RANK
    w_kr = w_in[:, o3:o3 + QK_ROPE_DIM]
    w_u = w_in[:, o3 + QK_ROPE_DIM:]
    w_kr_grp = jnp.concatenate([jnp.zeros((D_MODEL, QK_NOPE_DIM), F32), w_kr, _rot_cols(w_kr)], axis=-1)
    w_in_ext = jnp.concatenate([w_cq, w_ckv, w_kr_grp, w_u], axis=-1).astype(BF16)

    w_q_t = w_uq.T.astype(BF16)

    kv_w = w_ukv.reshape(KV_LORA_RANK, MLA_HEADS, QK_NOPE_DIM + V_HEAD_DIM)
    w_uv_t = kv_w[..., QK_NOPE_DIM:].reshape(KV_LORA_RANK, MLA_WIDTH).T.astype(BF16)

    w_f = jnp.zeros((FNET_WIDTH, FNET_WIDTH), F32)
    for gi in range(FNET_GROUPS):
        sl = slice(gi * FNET_GROUP_DIM, (gi + 1) * FNET_GROUP_DIM)
        w_f = w_f.at[sl, sl].set(w_fnet[gi])
    return w_in_ext, w_q_t, w_uv_t, w_f.astype(BF16)


def _dft_mats(n):
    idx = np.arange(n, dtype=np.int64)
    ang = ((idx[:, None] * idx[None, :]) % n) * (2.0 * np.pi / n)
    return np.cos(ang), np.sin(ang)


def _seq_dft_tables(seq):
    cs_rows, sn_rows = [], []
    n = seq
    for _ in range(DFT_LEVELS):
        ang = np.arange(n // 2) * (2.0 * np.pi / n)
        cs_rows.append(np.cos(ang))
        sn_rows.append(np.sin(ang))
        n //= 2
    rep = lambda rows: jnp.asarray(np.broadcast_to(np.concatenate(rows)[:, None], (seq - n, LANES)), F32)
    c, s = _dft_mats(n)
    leaf = jnp.asarray(np.concatenate([c, s], axis=1) * seq ** -0.5, F32).astype(BF16)
    return rep(cs_rows), rep(sn_rows), leaf


def _channel_dft():
    c, s = _dft_mats(FNET_GROUP_DIM)
    return jnp.asarray(np.concatenate([c, -s], axis=-1) * FNET_GROUP_DIM ** -0.5, F32).astype(BF16)


def _rope_tabs(seq):
    inv = 1.0 / (ROPE_BASE ** (np.arange(0, QK_ROPE_DIM, 2) / QK_ROPE_DIM))
    ang = np.arange(seq)[:, None] * inv[None, :]
    cos, sin = np.cos(ang), np.sin(ang)
    pad = lambda t: jnp.asarray(np.concatenate(
        [np.zeros((seq, QK_NOPE_DIM)), t, t, np.zeros((seq, HEAD_PAD - QK_NOPE_DIM - QK_ROPE_DIM))], axis=-1), F32)
    return pad(cos), pad(sin), jnp.asarray(cos.T, F32), jnp.asarray(sin.T, F32)


def _trunk(x, mem, w, tiles):
    S = x.shape[1]
    ctab, stab, cos_t, sin_t = _rope_tabs(S)
    twc, tws, cs = _seq_dft_tables(S)
    q_t, k, v_t, kn2, qn2, lb, vr, vi = _proj(x, ctab, stab, cos_t, sin_t, w["ln_mix_g"], w["w_in_ext"],
                                              w["q_norm_g"], w["w_q_t"], w["kv_norm_g"], w["w_ukv"],
                                              w["w_uv_t"], w["w_cdft"], min(tiles[0], S))
    shift, safe = _score_shifts(kn2, qn2, lb, min(tiles[1], S))
    o = _attn(q_t, k, v_t, shift, safe, min(tiles[1], S))
    g = _dft(vr, vi, twc, tws, cs, w["w_f"])
    return _post(x, o, g, mem, w["w_out"], w["ln_x_g"], w["mem_norm_g"], w["w_xq"], w["w_xkv"], w["w_xo"],
                 w["ln_ffn_g"], w["w_gate_up"], w["w_down"], w["final_norm_g"], min(tiles[2], S))


def kernel(x_prompt, x_sample, mem_prompt, mem_sample, ln_mix_g, w_in, q_norm_g, w_uq, kv_norm_g, w_ukv, w_fnet,
           w_out, ln_x_g, mem_norm_g, w_xq, w_xkv, w_xo, ln_ffn_g, w_gate_up, w_down, final_norm_g):
    assert ln_mix_g.shape[0] == 1, "single-layer trunk"
    w_in_ext, w_q_t, w_uv_t, w_f = _prep_weights(w_in[0], w_uq[0], w_ukv[0], w_fnet[0])
    row = lambda g: g.reshape(1, -1).astype(F32)
    w = dict(
        ln_mix_g=row(ln_mix_g[0]), w_in_ext=w_in_ext, q_norm_g=row(q_norm_g[0]), w_q_t=w_q_t,
        kv_norm_g=row(kv_norm_g[0]), w_ukv=w_ukv[0].astype(BF16), w_uv_t=w_uv_t, w_cdft=_channel_dft(), w_f=w_f,
        w_out=w_out[0].astype(BF16), ln_x_g=row(ln_x_g[0]), mem_norm_g=row(mem_norm_g[0]),
        w_xq=w_xq[0].astype(BF16), w_xkv=w_xkv[0].astype(BF16), w_xo=w_xo[0].astype(BF16),
        ln_ffn_g=row(ln_ffn_g[0]), w_gate_up=w_gate_up[0].astype(BF16), w_down=w_down[0].astype(BF16),
        final_norm_g=row(final_norm_g),
    )
    return (_trunk(x_prompt, mem_prompt, w, TILES), _trunk(x_sample, mem_sample, w, TILES))
```

```python
import functools

import numpy as np
import jax
import jax.numpy as jnp
from jax import lax
from jax.experimental import pallas as pl
from jax.experimental.pallas import tpu as pltpu

F32 = jnp.float32
BF16 = jnp.bfloat16

D_MODEL = 1024
MLA_HEADS = 8
QK_NOPE_DIM = 64
QK_ROPE_DIM = 32
V_HEAD_DIM = 64
Q_LORA_RANK = 384
KV_LORA_RANK = 256
FNET_GROUPS = 4
FNET_GROUP_DIM = 128
FNET_WIDTH = FNET_GROUPS * FNET_GROUP_DIM
MLA_WIDTH = MLA_HEADS * V_HEAD_DIM
XATTN_HEADS = 4
XATTN_HEAD_DIM = D_MODEL // XATTN_HEADS
D_FF = 2816
ROPE_BASE = 10000.0
NORM_EPS = 1e-6
LOG2_E = 1.4426950408889634

LANES = 128
SUBLANES = 8
HEAD_PAD = LANES
FF_CHUNK = 256
VMEM_LIMIT = 56 * 1024 * 1024
TILES = (1024, 512, 512)
ATTN_KEY_CHUNK = 1024
ATTN_SHIFT_MARGIN = 90.0
ATTN_MAX_BOUND = 75.0
ATTN_NORM_SLACK = 1.02
DFT_LEVELS = 3
DFT_ROW_CHUNK = 256

_O_CQ = 0
_O_CKV = _O_CQ + Q_LORA_RANK
_O_KR = _O_CKV + KV_LORA_RANK
_O_U = _O_KR + HEAD_PAD
IN_EXT = _O_U + FNET_WIDTH


def _rms(x, g):
    return x * lax.rsqrt(jnp.mean(x * x, axis=-1, keepdims=True) + NORM_EPS) * g


def _dot(a, b):
    return jnp.dot(a, b, preferred_element_type=F32)


def _dot_nt(a, b):
    return lax.dot_general(a, b, (((1,), (1,)), ((), ())), preferred_element_type=F32)


def _const_spec(shape):
    zeros = (0,) * len(shape)
    return pl.BlockSpec(shape, lambda *_: zeros, pipeline_mode=pl.Buffered(1))


def _params(n_axes):
    return pltpu.CompilerParams(dimension_semantics=("arbitrary",) * n_axes, vmem_limit_bytes=VMEM_LIMIT)


def _proj_kernel(x_ref, ctab_ref, stab_ref, cos_t_ref, sin_t_ref, ln_g_ref, w_in_ref, qn_g_ref, w_q_t_ref,
                 kvn_g_ref, w_ukv_ref, w_uv_t_ref, w_cdft_ref,
                 q_t_ref, k_ref, v_t_ref, kn2_ref, qn2_ref, vr_ref, vi_ref):
    scale = (QK_NOPE_DIM + QK_ROPE_DIM) ** -0.5 * LOG2_E
    half = QK_ROPE_DIM // 2
    qk_dim = QK_NOPE_DIM + QK_ROPE_DIM
    h = _rms(x_ref[0], ln_g_ref[...]).astype(BF16)
    z = _dot(h, w_in_ref[...])

    cq = _rms(z[:, _O_CQ:_O_CQ + Q_LORA_RANK], qn_g_ref[...]).astype(BF16)
    q_t = _dot_nt(w_q_t_ref[...], cq)
    cos_t = cos_t_ref[...] * scale
    sin_t = sin_t_ref[...] * scale
    zero_rows = jnp.zeros((HEAD_PAD - qk_dim, q_t.shape[1]), BF16)
    for hd in range(MLA_HEADS):
        src, dst = hd * qk_dim, hd * HEAD_PAD
        x1 = q_t[src + QK_NOPE_DIM:src + QK_NOPE_DIM + half, :]
        x2 = q_t[src + QK_NOPE_DIM + half:src + qk_dim, :]
        q_h = jnp.concatenate([q_t[src:src + QK_NOPE_DIM, :] * scale, x1 * cos_t - x2 * sin_t,
                               x2 * cos_t + x1 * sin_t], axis=0)
        q_t_ref[0, dst:dst + HEAD_PAD, :] = jnp.concatenate([q_h.astype(BF16), zero_rows], axis=0)
        qn2_ref[0, hd:hd + 1, :] = jnp.sum(q_h * q_h, axis=0, keepdims=True)

    ckv = _rms(z[:, _O_CKV:_O_CKV + KV_LORA_RANK], kvn_g_ref[...]).astype(BF16)
    kv = _dot(ckv, w_ukv_ref[...])
    zk = z[:, _O_KR:_O_KR + HEAD_PAD]
    kr = zk * ctab_ref[...] + pltpu.roll(zk, HEAD_PAD - QK_ROPE_DIM, 1) * stab_ref[...]
    nope = lax.broadcasted_iota(jnp.int32, kr.shape, 1) < QK_NOPE_DIM
    k_heads = [jnp.where(nope, kv[:, hd * HEAD_PAD:(hd + 1) * HEAD_PAD], kr) for hd in range(MLA_HEADS)]
    for hd in range(MLA_HEADS):
        k_ref[0, hd] = k_heads[hd].astype(BF16)
    lane1 = lax.broadcasted_iota(jnp.int32, (1, LANES), 1)
    kn2 = jnp.zeros((1, LANES), F32)
    for hd in range(MLA_HEADS):
        row_n2 = jnp.sum(k_heads[hd] * k_heads[hd], axis=1, keepdims=True)
        kn2 = jnp.where(lane1 == hd, jnp.max(row_n2, axis=0, keepdims=True), kn2)
    kn2_ref[0, 0] = jnp.broadcast_to(kn2, kn2_ref.shape[2:])
    v_t_ref[0] = _dot_nt(w_uv_t_ref[...], ckv).astype(BF16)

    for gi in range(FNET_GROUPS):
        sl = slice(gi * FNET_GROUP_DIM, (gi + 1) * FNET_GROUP_DIM)
        v = _dot(z[:, _O_U + gi * FNET_GROUP_DIM:_O_U + (gi + 1) * FNET_GROUP_DIM].astype(BF16), w_cdft_ref[...])
        vr_ref[0, :, sl] = v[:, :FNET_GROUP_DIM].astype(BF16)
        vi_ref[0, :, sl] = v[:, FNET_GROUP_DIM:].astype(BF16)


def _proj(x, ctab, stab, cos_t, sin_t, ln_g, w_in_ext, qn_g, w_q_t, kvn_g, w_ukv, w_uv_t, w_cdft, tile):
    B, S, _ = x.shape
    hp = MLA_HEADS * HEAD_PAD
    tok = lambda w: pl.BlockSpec((1, tile, w), lambda b, t: (b, t, 0))
    tok_t = lambda w: pl.BlockSpec((1, w, tile), lambda b, t: (b, 0, t))
    tab = pl.BlockSpec((tile, LANES), lambda b, t: (t, 0))
    tab_t = pl.BlockSpec((cos_t.shape[0], tile), lambda b, t: (0, t))
    return pl.pallas_call(
        _proj_kernel,
        grid=(B, S // tile),
        in_specs=[tok(D_MODEL), tab, tab, tab_t, tab_t, _const_spec(ln_g.shape), _const_spec(w_in_ext.shape),
                  _const_spec(qn_g.shape), _const_spec(w_q_t.shape), _const_spec(kvn_g.shape),
                  _const_spec(w_ukv.shape), _const_spec(w_uv_t.shape), _const_spec(w_cdft.shape)],
        out_specs=[tok_t(hp), pl.BlockSpec((1, MLA_HEADS, tile, HEAD_PAD), lambda b, t: (b, 0, t, 0)),
                   tok_t(MLA_WIDTH), pl.BlockSpec((1, 1, SUBLANES, LANES), lambda b, t: (b, t, 0, 0)),
                   tok_t(MLA_HEADS),
                   tok(FNET_WIDTH), tok(FNET_WIDTH)],
        out_shape=[jax.ShapeDtypeStruct((B, hp, S), BF16), jax.ShapeDtypeStruct((B, MLA_HEADS, S, HEAD_PAD), BF16),
                   jax.ShapeDtypeStruct((B, MLA_WIDTH, S), BF16), jax.ShapeDtypeStruct((B, S // tile, SUBLANES, LANES), F32),
                   jax.ShapeDtypeStruct((B, MLA_HEADS, S), F32)]
                  + [jax.ShapeDtypeStruct((B, S, FNET_WIDTH), BF16)] * 2,
        compiler_params=_params(2),
        name="proj",
    )(x, ctab, stab, cos_t, sin_t, ln_g, w_in_ext, qn_g, w_q_t, kvn_g, w_ukv, w_uv_t, w_cdft)


def _attn_kernel(safe_ref, q_t_ref, k_ref, v_t_ref, shift_ref, o_ref, s_scr):
    S = k_ref.shape[2]
    kc = min(ATTN_KEY_CHUNK, S)
    n_chunks = S // kc
    heads = range(MLA_HEADS)

    def k_blk(hh, r0, r1):
        return k_ref[0, hh, r0:r1, :]

    def q_blk(hh):
        return q_t_ref[0, hh * HEAD_PAD:(hh + 1) * HEAD_PAD, :]

    def v_blk(hh, c):
        return v_t_ref[0, hh * V_HEAD_DIM:(hh + 1) * V_HEAD_DIM, c * kc:(c + 1) * kc]

    def finish(acc, den):
        o_ref[0] = jnp.concatenate([a / l for a, l in zip(acc, den)], axis=0).T.astype(BF16)

    all_safe = safe_ref[pl.program_id(0), pl.program_id(1)] != 0

    @pl.when(all_safe)
    def _single_pass():
        items = [(hh, c) for hh in heads for c in range(n_chunks)]
        score = lambda hh, c: _dot(k_blk(hh, c * kc, (c + 1) * kc), q_blk(hh))
        acc, den = [None] * len(heads), [None] * len(heads)
        s_next = score(*items[0])
        for i, (hh, c) in enumerate(items):
            s = s_next
            if i + 1 < len(items):
                s_next = score(*items[i + 1])
            p = jnp.exp2(s - shift_ref[0, hh:hh + 1, :])
            o_c, l_c = _dot(v_blk(hh, c), p.astype(BF16)), jnp.sum(p, axis=0, keepdims=True)
            acc[hh] = o_c if acc[hh] is None else acc[hh] + o_c
            den[hh] = l_c if den[hh] is None else den[hh] + l_c
        finish(acc, den)

    @pl.when(jnp.logical_not(all_safe))
    def _two_pass():
        def score_chunk(hh, c):
            s = _dot(k_blk(hh, c * kc, (c + 1) * kc), q_blk(hh))
            s_scr[hh % 2, c * kc:(c + 1) * kc, :] = s
            return jnp.max(s, axis=0, keepdims=True)

        items = [(hh, c) for hh in heads for c in range(n_chunks)]
        m_parts = [[] for _ in heads]
        m_head, acc, den = [None] * len(heads), [None] * len(heads), [None] * len(heads)
        for i in range(len(items) + n_chunks):
            if i < len(items):
                hh, c = items[i]
                m_parts[hh].append(score_chunk(hh, c))
                if c == n_chunks - 1:
                    m_head[hh] = functools.reduce(jnp.maximum, m_parts[hh])
            if i >= n_chunks:
                hh, c = items[i - n_chunks]
                p = jnp.exp2(s_scr[hh % 2, c * kc:(c + 1) * kc, :] - m_head[hh])
                o_c, l_c = _dot(v_blk(hh, c), p.astype(BF16)), jnp.sum(p, axis=0, keepdims=True)
                acc[hh] = o_c if acc[hh] is None else acc[hh] + o_c
                den[hh] = l_c if den[hh] is None else den[hh] + l_c
        finish(acc, den)


def _score_shifts(kn2, qn2, tile):
    B, _, S = qn2.shape
    kmax2 = jnp.max(kn2[:, :, 0, :MLA_HEADS], axis=1)
    ub = jnp.sqrt(qn2 * kmax2[:, :, None]) * ATTN_NORM_SLACK
    ok = ub <= ATTN_MAX_BOUND
    safe = jnp.all(ok.reshape(B, MLA_HEADS, S // tile, tile), axis=(1, 3))
    return ATTN_SHIFT_MARGIN - ub, safe.astype(jnp.int32)


def _attn(q_t, k, v_t, shift, safe, tile):
    B, _, S, _ = k.shape
    hp = MLA_HEADS * HEAD_PAD
    return pl.pallas_call(
        _attn_kernel,
        grid_spec=pltpu.PrefetchScalarGridSpec(
            num_scalar_prefetch=1,
            grid=(B, S // tile),
            in_specs=[pl.BlockSpec((1, hp, tile), lambda b, t, safe: (b, 0, t)),
                      pl.BlockSpec((1, MLA_HEADS, S, HEAD_PAD), lambda b, t, safe: (b, 0, 0, 0)),
                      pl.BlockSpec((1, MLA_WIDTH, S), lambda b, t, safe: (b, 0, 0)),
                      pl.BlockSpec((1, MLA_HEADS, tile), lambda b, t, safe: (b, 0, t))],
            out_specs=pl.BlockSpec((1, tile, MLA_WIDTH), lambda b, t, safe: (b, t, 0)),
            scratch_shapes=[pltpu.VMEM((2, S, tile), F32)]),
        out_shape=jax.ShapeDtypeStruct((B, S, MLA_WIDTH), BF16),
        compiler_params=_params(2),
        name="attn",
    )(safe, q_t, k, v_t, shift)


def _bit_reverse(i, bits):
    return int(format(i, "0%db" % bits)[::-1], 2)


def _dft_kernel(vr_ref, vi_ref, twc_ref, tws_ref, cs_ref, w_f_ref, g_ref, xr_scr, xi_scr, xb_scr):
    S = vr_ref.shape[1]
    n_leaves = 2 ** DFT_LEVELS
    leaf = S // n_leaves
    chunk = min(DFT_ROW_CHUNK, leaf)
    reps = FNET_WIDTH // LANES
    n, off = S, 0
    for lev in range(DFT_LEVELS):
        half = n // 2
        last = lev == DFT_LEVELS - 1
        for base in range(0, S, n):
            for c0 in range(0, half, chunk):
                top = slice(base + c0, base + c0 + chunk)
                bot = slice(base + half + c0, base + half + c0 + chunk)
                if lev == 0:
                    tr, ti = vr_ref[0, top, :].astype(F32), vi_ref[0, top, :].astype(F32)
                    br, bi = vr_ref[0, bot, :].astype(F32), vi_ref[0, bot, :].astype(F32)
                else:
                    tr, ti, br, bi = xr_scr[top, :], xi_scr[top, :], xr_scr[bot, :], xi_scr[bot, :]
                c = jnp.concatenate([twc_ref[off + c0:off + c0 + chunk, :]] * reps, axis=1)
                s = jnp.concatenate([tws_ref[off + c0:off + c0 + chunk, :]] * reps, axis=1)
                dr, di = tr - br, ti - bi
                ar, ai = tr + br, ti + bi
                mr, mi = dr * c + di * s, di * c - dr * s
                if last:
                    blk, r0 = (base // n) * 2, c0
                    xb_scr[blk, r0:r0 + chunk, :] = ar.astype(BF16)
                    xb_scr[blk, leaf + r0:leaf + r0 + chunk, :] = ai.astype(BF16)
                    xb_scr[blk + 1, r0:r0 + chunk, :] = mr.astype(BF16)
                    xb_scr[blk + 1, leaf + r0:leaf + r0 + chunk, :] = mi.astype(BF16)
                else:
                    xr_scr[top, :], xi_scr[top, :] = ar, ai
                    xr_scr[bot, :], xi_scr[bot, :] = mr, mi
        off += half
        n = half
    for blk in range(n_leaves):
        f = _dot(cs_ref[...], xb_scr[blk])
        g = _dot(f.astype(BF16), w_f_ref[...])
        g_ref[0, _bit_reverse(blk, DFT_LEVELS)] = g.astype(BF16)


def _dft(vr, vi, twc, tws, cs, w_f):
    B, S, _ = vr.shape
    n_leaves = 2 ** DFT_LEVELS
    leaf = S // n_leaves
    bat = pl.BlockSpec((1, S, FNET_WIDTH), lambda b: (b, 0, 0))
    return pl.pallas_call(
        _dft_kernel,
        grid=(B,),
        in_specs=[bat, bat, _const_spec(twc.shape), _const_spec(tws.shape), _const_spec(cs.shape),
                  _const_spec(w_f.shape)],
        out_specs=pl.BlockSpec((1, n_leaves, leaf, FNET_WIDTH), lambda b: (b, 0, 0, 0)),
        out_shape=jax.ShapeDtypeStruct((B, n_leaves, leaf, FNET_WIDTH), BF16),
        scratch_shapes=[pltpu.VMEM((S, FNET_WIDTH), F32), pltpu.VMEM((S, FNET_WIDTH), F32),
                        pltpu.VMEM((n_leaves, 2 * leaf, FNET_WIDTH), BF16)],
        compiler_params=_params(1),
        name="dft",
    )(vr, vi, twc, tws, cs, w_f)


def _post_kernel(x_ref, o_ref, g_ref, mem_ref, w_out_ref, ln_x_g_ref, mem_g_ref, w_xq_ref, w_xkv_ref,
                 w_xo_ref, ln_f_g_ref, w_gu_ref, w_d_ref, fin_g_ref, y_ref, kv_scr, a_scr, g_scr):
    @pl.when(pl.program_id(1) == 0)
    def _():
        m = _rms(mem_ref[0], mem_g_ref[...]).astype(BF16)
        kv_scr[...] = _dot(m, w_xkv_ref[...]).astype(BF16)

    n_res, rows_per_res = g_ref.shape[1], g_ref.shape[2]
    for r in range(n_res):
        g_r = g_ref[0, r].astype(F32)
        for lt in range(FNET_WIDTH // LANES):
            g_scr[lt, pl.ds(r, rows_per_res, stride=n_res), :] = g_r[:, lt * LANES:(lt + 1) * LANES]
    g = jnp.concatenate([g_scr[lt] for lt in range(FNET_WIDTH // LANES)], axis=1).astype(BF16)

    x1 = x_ref[0] + _dot(jnp.concatenate([o_ref[0], g], axis=1), w_out_ref[...])

    hq = _rms(x1, ln_x_g_ref[...]).astype(BF16)
    q = (_dot(hq, w_xq_ref[...]) * (XATTN_HEAD_DIM ** -0.5 * LOG2_E)).astype(BF16)
    heads = []
    for hd in range(XATTN_HEADS):
        sl = slice(hd * XATTN_HEAD_DIM, (hd + 1) * XATTN_HEAD_DIM)
        s = _dot_nt(q[:, sl], kv_scr[:, sl])
        m = jnp.max(s, axis=-1, keepdims=True)
        p = jnp.exp2(s - m)
        l = jnp.sum(p, axis=-1, keepdims=True)
        vh = kv_scr[:, D_MODEL + hd * XATTN_HEAD_DIM:D_MODEL + (hd + 1) * XATTN_HEAD_DIM]
        heads.append((_dot(p.astype(BF16), vh) / l).astype(BF16))
    x2 = x1 + _dot(jnp.concatenate(heads, axis=-1), w_xo_ref[...])

    hf = _rms(x2, ln_f_g_ref[...]).astype(BF16)
    for c in range(D_FF // FF_CHUNK):
        sl = slice(c * FF_CHUNK, (c + 1) * FF_CHUNK)
        gate = _dot(hf, w_gu_ref[:, sl])
        up = _dot(hf, w_gu_ref[:, D_FF + c * FF_CHUNK:D_FF + (c + 1) * FF_CHUNK])
        a_scr[:, sl] = (gate * jax.nn.sigmoid(gate) * up).astype(BF16)
    x3 = x2 + _dot(a_scr[...], w_d_ref[...])
    y_ref[0] = _rms(x3, fin_g_ref[...])


def _post(x, o, g, mem, w_out, ln_x_g, mem_g, w_xq, w_xkv, w_xo, ln_f_g, w_gu, w_d, fin_g, tile):
    B, S, _ = x.shape
    M = mem.shape[1]
    n_res = g.shape[1]
    tok = lambda w: pl.BlockSpec((1, tile, w), lambda b, t: (b, t, 0))
    consts = [w_out, ln_x_g, mem_g, w_xq, w_xkv, w_xo, ln_f_g, w_gu, w_d, fin_g]
    return pl.pallas_call(
        _post_kernel,
        grid=(B, S // tile),
        in_specs=[tok(D_MODEL), tok(MLA_WIDTH),
                  pl.BlockSpec((1, n_res, tile // n_res, FNET_WIDTH), lambda b, t: (b, 0, t, 0)),
                  pl.BlockSpec((1, M, D_MODEL), lambda b, t: (b, 0, 0))]
                 + [_const_spec(c.shape) for c in consts],
        out_specs=tok(D_MODEL),
        out_shape=jax.ShapeDtypeStruct((B, S, D_MODEL), F32),
        scratch_shapes=[pltpu.VMEM((M, 2 * D_MODEL), BF16), pltpu.VMEM((tile, D_FF), BF16),
                        pltpu.VMEM((FNET_WIDTH // LANES, tile, LANES), F32)],
        compiler_params=_params(2),
        name="post",
    )(x, o, g, mem, *consts)


def _rot_cols(w):
    half = QK_ROPE_DIM // 2
    return jnp.concatenate([-w[..., half:], w[..., :half]], axis=-1)


def _prep_weights(w_in, w_uq, w_ukv, w_fnet):
    w_cq = w_in[:, :Q_LORA_RANK]
    w_ckv = w_in[:, Q_LORA_RANK:Q_LORA_RANK + KV_LORA_RANK]
    o3 = Q_LORA_RANK + KV_LORA_RANK
    w_kr = w_in[:, o3:o3 + QK_ROPE_DIM]
    w_u = w_in[:, o3 + QK_ROPE_DIM:]
    w_kr_grp = jnp.concatenate([jnp.zeros((D_MODEL, QK_NOPE_DIM), F32), w_kr, _rot_cols(w_kr)], axis=-1)
    w_in_ext = jnp.concatenate([w_cq, w_ckv, w_kr_grp, w_u], axis=-1).astype(BF16)

    w_q_t = w_uq.T.astype(BF16)

    kv_w = w_ukv.reshape(KV_LORA_RANK, MLA_HEADS, QK_NOPE_DIM + V_HEAD_DIM)
    w_uv_t = kv_w[..., QK_NOPE_DIM:].reshape(KV_LORA_RANK, MLA_WIDTH).T.astype(BF16)

    w_f = jnp.zeros((FNET_WIDTH, FNET_WIDTH), F32)
    for gi in range(FNET_GROUPS):
        sl = slice(gi * FNET_GROUP_DIM, (gi + 1) * FNET_GROUP_DIM)
        w_f = w_f.at[sl, sl].set(w_fnet[gi])
    return w_in_ext, w_q_t, w_uv_t, w_f.astype(BF16)


def _dft_mats(n):
    idx = np.arange(n, dtype=np.int64)
    ang = ((idx[:, None] * idx[None, :]) % n) * (2.0 * np.pi / n)
    return np.cos(ang), np.sin(ang)


def _seq_dft_tables(seq):
    cs_rows, sn_rows = [], []
    n = seq
    for _ in range(DFT_LEVELS):
        ang = np.arange(n // 2) * (2.0 * np.pi / n)
        cs_rows.append(np.cos(ang))
        sn_rows.append(np.sin(ang))
        n //= 2
    rep = lambda rows: jnp.asarray(np.broadcast_to(np.concatenate(rows)[:, None], (seq - n, LANES)), F32)
    c, s = _dft_mats(n)
    leaf = jnp.asarray(np.concatenate([c, s], axis=1) * seq ** -0.5, F32).astype(BF16)
    return rep(cs_rows), rep(sn_rows), leaf


def _channel_dft():
    c, s = _dft_mats(FNET_GROUP_DIM)
    return jnp.asarray(np.concatenate([c, -s], axis=-1) * FNET_GROUP_DIM ** -0.5, F32).astype(BF16)


def _rope_tabs(seq):
    inv = 1.0 / (ROPE_BASE ** (np.arange(0, QK_ROPE_DIM, 2) / QK_ROPE_DIM))
    ang = np.arange(seq)[:, None] * inv[None, :]
    cos, sin = np.cos(ang), np.sin(ang)
    pad = lambda t: jnp.asarray(np.concatenate(
        [np.zeros((seq, QK_NOPE_DIM)), t, t, np.zeros((seq, HEAD_PAD - QK_NOPE_DIM - QK_ROPE_DIM))], axis=-1), F32)
    return pad(cos), pad(sin), jnp.asarray(cos.T, F32), jnp.asarray(sin.T, F32)


def _trunk(x, mem, w, tiles):
    S = x.shape[1]
    ctab, stab, cos_t, sin_t = _rope_tabs(S)
    twc, tws, cs = _seq_dft_tables(S)
    q_t, k, v_t, kn2, qn2, vr, vi = _proj(x, ctab, stab, cos_t, sin_t, w["ln_mix_g"], w["w_in_ext"],
                                          w["q_norm_g"], w["w_q_t"], w["kv_norm_g"], w["w_ukv"],
                                          w["w_uv_t"], w["w_cdft"], min(tiles[0], S))
    shift, safe = _score_shifts(kn2, qn2, min(tiles[1], S))
    o = _attn(q_t, k, v_t, shift, safe, min(tiles[1], S))
    g = _dft(vr, vi, twc, tws, cs, w["w_f"])
    return _post(x, o, g, mem, w["w_out"], w["ln_x_g"], w["mem_norm_g"], w["w_xq"], w["w_xkv"], w["w_xo"],
                 w["ln_ffn_g"], w["w_gate_up"], w["w_down"], w["final_norm_g"], min(tiles[2], S))


def kernel(x_prompt, x_sample, mem_prompt, mem_sample, ln_mix_g, w_in, q_norm_g, w_uq, kv_norm_g, w_ukv, w_fnet,
           w_out, ln_x_g, mem_norm_g, w_xq, w_xkv, w_xo, ln_ffn_g, w_gate_up, w_down, final_norm_g):
    assert ln_mix_g.shape[0] == 1, "single-layer trunk"
    w_in_ext, w_q_t, w_uv_t, w_f = _prep_weights(w_in[0], w_uq[0], w_ukv[0], w_fnet[0])
    row = lambda g: g.reshape(1, -1).astype(F32)
    w = dict(
        ln_mix_g=row(ln_mix_g[0]), w_in_ext=w_in_ext, q_norm_g=row(q_norm_g[0]), w_q_t=w_q_t,
        kv_norm_g=row(kv_norm_g[0]), w_ukv=w_ukv[0].astype(BF16), w_uv_t=w_uv_t, w_cdft=_channel_dft(), w_f=w_f,
        w_out=w_out[0].astype(BF16), ln_x_g=row(ln_x_g[0]), mem_norm_g=row(mem_norm_g[0]),
        w_xq=w_xq[0].astype(BF16), w_xkv=w_xkv[0].astype(BF16), w_xo=w_xo[0].astype(BF16),
        ln_ffn_g=row(ln_ffn_g[0]), w_gate_up=w_gate_up[0].astype(BF16), w_down=w_down[0].astype(BF16),
        final_norm_g=row(final_norm_g),
    )
    return (_trunk(x_prompt, mem_prompt, w, TILES), _trunk(x_sample, mem_sample, w, TILES))
```

```python
import functools

import numpy as np
import jax
import jax.numpy as jnp
from jax import lax
from jax.experimental import pallas as pl
from jax.experimental.pallas import tpu as pltpu

F32 = jnp.float32
BF16 = jnp.bfloat16

D_MODEL = 1024
MLA_HEADS = 8
QK_NOPE_DIM = 64
QK_ROPE_DIM = 32
V_HEAD_DIM = 64
Q_LORA_RANK = 384
KV_LORA_RANK = 256
FNET_GROUPS = 4
FNET_GROUP_DIM = 128
FNET_WIDTH = FNET_GROUPS * FNET_GROUP_DIM
MLA_WIDTH = MLA_HEADS * V_HEAD_DIM
XATTN_HEADS = 4
XATTN_HEAD_DIM = D_MODEL // XATTN_HEADS
D_FF = 2816
ROPE_BASE = 10000.0
NORM_EPS = 1e-6
LOG2_E = 1.4426950408889634

LANES = 128
SUBLANES = 8
HEAD_PAD = LANES
FF_CHUNK = 256
VMEM_LIMIT = 56 * 1024 * 1024
TILES = (1024, 512, 512)
ATTN_KEY_CHUNK = 1024
ATTN_SHIFT_MARGIN = 90.0
ATTN_MAX_BOUND = 75.0
ATTN_NORM_SLACK = 1.02
DFT_LEVELS = 3
DFT_ROW_CHUNK = 256

_O_CQ = 0
_O_CKV = _O_CQ + Q_LORA_RANK
_O_KR = _O_CKV + KV_LORA_RANK
_O_U = _O_KR + HEAD_PAD
IN_EXT = _O_U + FNET_WIDTH


def _rms(x, g):
    return x * lax.rsqrt(jnp.mean(x * x, axis=-1, keepdims=True) + NORM_EPS) * g


def _dot(a, b):
    return jnp.dot(a, b, preferred_element_type=F32)


def _dot_nt(a, b):
    return lax.dot_general(a, b, (((1,), (1,)), ((), ())), preferred_element_type=F32)


def _const_spec(shape):
    zeros = (0,) * len(shape)
    return pl.BlockSpec(shape, lambda *_: zeros, pipeline_mode=pl.Buffered(1))


def _params(n_axes):
    return pltpu.CompilerParams(dimension_semantics=("arbitrary",) * n_axes, vmem_limit_bytes=VMEM_LIMIT)


def _proj_kernel(x_ref, ctab_ref, stab_ref, cos_t_ref, sin_t_ref, ln_g_ref, w_in_ref, qn_g_ref, w_q_t_ref,
                 kvn_g_ref, w_ukv_ref, w_uv_t_ref, w_cdft_ref,
                 q_t_ref, k_ref, v_t_ref, kn2_ref, qn2_ref, vr_ref, vi_ref):
    scale = (QK_NOPE_DIM + QK_ROPE_DIM) ** -0.5 * LOG2_E
    half = QK_ROPE_DIM // 2
    qk_dim = QK_NOPE_DIM + QK_ROPE_DIM
    h = _rms(x_ref[0], ln_g_ref[...]).astype(BF16)
    z = _dot(h, w_in_ref[...])

    cq = _rms(z[:, _O_CQ:_O_CQ + Q_LORA_RANK], qn_g_ref[...]).astype(BF16)
    q_t = _dot_nt(w_q_t_ref[...], cq)
    cos_t = cos_t_ref[...] * scale
    sin_t = sin_t_ref[...] * scale
    zero_rows = jnp.zeros((HEAD_PAD - qk_dim, q_t.shape[1]), BF16)
    for hd in range(MLA_HEADS):
        src, dst = hd * qk_dim, hd * HEAD_PAD
        x1 = q_t[src + QK_NOPE_DIM:src + QK_NOPE_DIM + half, :]
        x2 = q_t[src + QK_NOPE_DIM + half:src + qk_dim, :]
        q_h = jnp.concatenate([q_t[src:src + QK_NOPE_DIM, :] * scale, x1 * cos_t - x2 * sin_t,
                               x2 * cos_t + x1 * sin_t], axis=0)
        q_t_ref[0, dst:dst + HEAD_PAD, :] = jnp.concatenate([q_h.astype(BF16), zero_rows], axis=0)
        qn2_ref[0, hd:hd + 1, :] = jnp.sum(q_h * q_h, axis=0, keepdims=True)

    ckv = _rms(z[:, _O_CKV:_O_CKV + KV_LORA_RANK], kvn_g_ref[...]).astype(BF16)
    kv = _dot(ckv, w_ukv_ref[...])
    zk = z[:, _O_KR:_O_KR + HEAD_PAD]
    kr = zk * ctab_ref[...] + pltpu.roll(zk, HEAD_PAD - QK_ROPE_DIM, 1) * stab_ref[...]
    nope = lax.broadcasted_iota(jnp.int32, kr.shape, 1) < QK_NOPE_DIM
    k_heads = [jnp.where(nope, kv[:, hd * HEAD_PAD:(hd + 1) * HEAD_PAD], kr) for hd in range(MLA_HEADS)]
    for hd in range(MLA_HEADS):
        k_ref[0, hd] = k_heads[hd].astype(BF16)
    lane1 = lax.broadcasted_iota(jnp.int32, (1, LANES), 1)
    kn2 = jnp.zeros((1, LANES), F32)
    for hd in range(MLA_HEADS):
        row_n2 = jnp.sum(k_heads[hd] * k_heads[hd], axis=1, keepdims=True)
        kn2 = jnp.where(lane1 == hd, jnp.max(row_n2, axis=0, keepdims=True), kn2)
    kn2_ref[0, 0] = jnp.broadcast_to(kn2, kn2_ref.shape[2:])
    v_t_ref[0] = _dot_nt(w_uv_t_ref[...], ckv).astype(BF16)

    for gi in range(FNET_GROUPS):
        sl = slice(gi * FNET_GROUP_DIM, (gi + 1) * FNET_GROUP_DIM)
        v = _dot(z[:, _O_U + gi * FNET_GROUP_DIM:_O_U + (gi + 1) * FNET_GROUP_DIM].astype(BF16), w_cdft_ref[...])
        vr_ref[0, :, sl] = v[:, :FNET_GROUP_DIM].astype(BF16)
        vi_ref[0, :, sl] = v[:, FNET_GROUP_DIM:].astype(BF16)


def _proj(x, ctab, stab, cos_t, sin_t, ln_g, w_in_ext, qn_g, w_q_t, kvn_g, w_ukv, w_uv_t, w_cdft, tile):
    B, S, _ = x.shape
    hp = MLA_HEADS * HEAD_PAD
    tok = lambda w: pl.BlockSpec((1, tile, w), lambda b, t: (b, t, 0))
    tok_t = lambda w: pl.BlockSpec((1, w, tile), lambda b, t: (b, 0, t))
    tab = pl.BlockSpec((tile, LANES), lambda b, t: (t, 0))
    tab_t = pl.BlockSpec((cos_t.shape[0], tile), lambda b, t: (0, t))
    return pl.pallas_call(
        _proj_kernel,
        grid=(B, S // tile),
        in_specs=[tok(D_MODEL), tab, tab, tab_t, tab_t, _const_spec(ln_g.shape), _const_spec(w_in_ext.shape),
                  _const_spec(qn_g.shape), _const_spec(w_q_t.shape), _const_spec(kvn_g.shape),
                  _const_spec(w_ukv.shape), _const_spec(w_uv_t.shape), _const_spec(w_cdft.shape)],
        out_specs=[tok_t(hp), pl.BlockSpec((1, MLA_HEADS, tile, HEAD_PAD), lambda b, t: (b, 0, t, 0)),
                   tok_t(MLA_WIDTH), pl.BlockSpec((1, 1, SUBLANES, LANES), lambda b, t: (b, t, 0, 0)),
                   tok_t(MLA_HEADS),
                   tok(FNET_WIDTH), tok(FNET_WIDTH)],
        out_shape=[jax.ShapeDtypeStruct((B, hp, S), BF16), jax.ShapeDtypeStruct((B, MLA_HEADS, S, HEAD_PAD), BF16),
                   jax.ShapeDtypeStruct((B, MLA_WIDTH, S), BF16), jax.ShapeDtypeStruct((B, S // tile, SUBLANES, LANES), F32),
                   jax.ShapeDtypeStruct((B, MLA_HEADS, S), F32)]
                  + [jax.ShapeDtypeStruct((B, S, FNET_WIDTH), BF16)] * 2,
        compiler_params=_params(2),
        name="proj",
    )(x, ctab, stab, cos_t, sin_t, ln_g, w_in_ext, qn_g, w_q_t, kvn_g, w_ukv, w_uv_t, w_cdft)


def _attn_kernel(safe_ref, q_t_ref, k_ref, v_t_ref, shift_ref, o_ref, s_scr):
    S = k_ref.shape[2]
    kc = min(ATTN_KEY_CHUNK, S)
    n_chunks = S // kc
    heads = range(MLA_HEADS)

    def k_blk(hh, r0, r1):
        return k_ref[0, hh, r0:r1, :]

    def q_blk(hh):
        return q_t_ref[0, hh * HEAD_PAD:(hh + 1) * HEAD_PAD, :]

    def v_blk(hh, c):
        return v_t_ref[0, hh * V_HEAD_DIM:(hh + 1) * V_HEAD_DIM, c * kc:(c + 1) * kc]

    def finish(acc, den):
        o_ref[0] = jnp.concatenate([a / l for a, l in zip(acc, den)], axis=0).T.astype(BF16)

    all_safe = safe_ref[pl.program_id(0), pl.program_id(1)] != 0

    @pl.when(all_safe)
    def _single_pass():
        items = [(hh, c) for hh in heads for c in range(n_chunks)]
        score = lambda hh, c: _dot(k_blk(hh, c * kc, (c + 1) * kc), q_blk(hh))
        acc, den = [None] * len(heads), [None] * len(heads)
        s_next = score(*items[0])
        for i, (hh, c) in enumerate(items):
            s = s_next
            if i + 1 < len(items):
                s_next = score(*items[i + 1])
            p = jnp.exp2(s - shift_ref[0, hh:hh + 1, :])
            o_c, l_c = _dot(v_blk(hh, c), p.astype(BF16)), jnp.sum(p, axis=0, keepdims=True)
            acc[hh] = o_c if acc[hh] is None else acc[hh] + o_c
            den[hh] = l_c if den[hh] is None else den[hh] + l_c
        finish(acc, den)

    @pl.when(jnp.logical_not(all_safe))
    def _two_pass():
        def score_chunk(hh, c):
            s = _dot(k_blk(hh, c * kc, (c + 1) * kc), q_blk(hh))
            s_scr[hh % 2, c * kc:(c + 1) * kc, :] = s
            return jnp.max(s, axis=0, keepdims=True)

        items = [(hh, c) for hh in heads for c in range(n_chunks)]
        m_parts = [[] for _ in heads]
        m_head, acc, den = [None] * len(heads), [None] * len(heads), [None] * len(heads)
        for i in range(len(items) + n_chunks):
            if i < len(items):
                hh, c = items[i]
                m_parts[hh].append(score_chunk(hh, c))
                if c == n_chunks - 1:
                    m_head[hh] = functools.reduce(jnp.maximum, m_parts[hh])
            if i >= n_chunks:
                hh, c = items[i - n_chunks]
                p = jnp.exp2(s_scr[hh % 2, c * kc:(c + 1) * kc, :] - m_head[hh])
                o_c, l_c = _dot(v_blk(hh, c), p.astype(BF16)), jnp.sum(p, axis=0, keepdims=True)
                acc[hh] = o_c if acc[hh] is None else acc[hh] + o_c
                den[hh] = l_c if den[hh] is None else den[hh] + l_c
        finish(acc, den)


def _score_shifts(kn2, qn2, tile):
    B, _, S = qn2.shape
    kmax2 = jnp.max(kn2[:, :, 0, :MLA_HEADS], axis=1)
    ub = jnp.sqrt(qn2 * kmax2[:, :, None]) * ATTN_NORM_SLACK
    ok = ub <= ATTN_MAX_BOUND
    safe = jnp.all(ok.reshape(B, MLA_HEADS, S // tile, tile), axis=(1, 3))
    return ATTN_SHIFT_MARGIN - ub, safe.astype(jnp.int32)


def _attn(q_t, k, v_t, shift, safe, tile):
    B, _, S, _ = k.shape
    hp = MLA_HEADS * HEAD_PAD
    return pl.pallas_call(
        _attn_kernel,
        grid_spec=pltpu.PrefetchScalarGridSpec(
            num_scalar_prefetch=1,
            grid=(B, S // tile),
            in_specs=[pl.BlockSpec((1, hp, tile), lambda b, t, safe: (b, 0, t)),
                      pl.BlockSpec((1, MLA_HEADS, S, HEAD_PAD), lambda b, t, safe: (b, 0, 0, 0)),
                      pl.BlockSpec((1, MLA_WIDTH, S), lambda b, t, safe: (b, 0, 0)),
                      pl.BlockSpec((1, MLA_HEADS, tile), lambda b, t, safe: (b, 0, t))],
            out_specs=pl.BlockSpec((1, tile, MLA_WIDTH), lambda b, t, safe: (b, t, 0)),
            scratch_shapes=[pltpu.VMEM((2, S, tile), F32)]),
        out_shape=jax.ShapeDtypeStruct((B, S, MLA_WIDTH), BF16),
        compiler_params=_params(2),
        name="attn",
    )(safe, q_t, k, v_t, shift)


def _bit_reverse(i, bits):
    return int(format(i, "0%db" % bits)[::-1], 2)


def _dft_kernel(vr_ref, vi_ref, twc_ref, tws_ref, cs_ref, w_f_ref, mem_ref, mem_g_ref, w_xkv_t_ref,
                g_ref, mkv_t_ref, xr_scr, xi_scr, xb_scr):
    m = _rms(mem_ref[0], mem_g_ref[...]).astype(BF16)
    mkv_t_ref[0] = _dot_nt(w_xkv_t_ref[...], m).astype(BF16)

    S = vr_ref.shape[1]
    n_leaves = 2 ** DFT_LEVELS
    leaf = S // n_leaves
    chunk = min(DFT_ROW_CHUNK, leaf)
    reps = FNET_WIDTH // LANES
    n, off = S, 0
    for lev in range(DFT_LEVELS):
        half = n // 2
        last = lev == DFT_LEVELS - 1
        for base in range(0, S, n):
            for c0 in range(0, half, chunk):
                top = slice(base + c0, base + c0 + chunk)
                bot = slice(base + half + c0, base + half + c0 + chunk)
                if lev == 0:
                    tr, ti = vr_ref[0, top, :].astype(F32), vi_ref[0, top, :].astype(F32)
                    br, bi = vr_ref[0, bot, :].astype(F32), vi_ref[0, bot, :].astype(F32)
                else:
                    tr, ti, br, bi = xr_scr[top, :], xi_scr[top, :], xr_scr[bot, :], xi_scr[bot, :]
                c = jnp.concatenate([twc_ref[off + c0:off + c0 + chunk, :]] * reps, axis=1)
                s = jnp.concatenate([tws_ref[off + c0:off + c0 + chunk, :]] * reps, axis=1)
                dr, di = tr - br, ti - bi
                ar, ai = tr + br, ti + bi
                mr, mi = dr * c + di * s, di * c - dr * s
                if last:
                    blk, r0 = (base // n) * 2, c0
                    xb_scr[blk, r0:r0 + chunk, :] = ar.astype(BF16)
                    xb_scr[blk, leaf + r0:leaf + r0 + chunk, :] = ai.astype(BF16)
                    xb_scr[blk + 1, r0:r0 + chunk, :] = mr.astype(BF16)
                    xb_scr[blk + 1, leaf + r0:leaf + r0 + chunk, :] = mi.astype(BF16)
                else:
                    xr_scr[top, :], xi_scr[top, :] = ar, ai
                    xr_scr[bot, :], xi_scr[bot, :] = mr, mi
        off += half
        n = half
    for blk in range(n_leaves):
        f = _dot(cs_ref[...], xb_scr[blk])
        g = _dot(f.astype(BF16), w_f_ref[...])
        g_ref[0, _bit_reverse(blk, DFT_LEVELS)] = g.astype(BF16)


def _dft(vr, vi, twc, tws, cs, w_f, mem, mem_g, w_xkv_t):
    B, S, _ = vr.shape
    M = mem.shape[1]
    n_leaves = 2 ** DFT_LEVELS
    leaf = S // n_leaves
    bat = pl.BlockSpec((1, S, FNET_WIDTH), lambda b: (b, 0, 0))
    return pl.pallas_call(
        _dft_kernel,
        grid=(B,),
        in_specs=[bat, bat, _const_spec(twc.shape), _const_spec(tws.shape), _const_spec(cs.shape),
                  _const_spec(w_f.shape), pl.BlockSpec((1, M, D_MODEL), lambda b: (b, 0, 0)),
                  _const_spec(mem_g.shape), _const_spec(w_xkv_t.shape)],
        out_specs=[pl.BlockSpec((1, n_leaves, leaf, FNET_WIDTH), lambda b: (b, 0, 0, 0)),
                   pl.BlockSpec((1, 2 * D_MODEL, M), lambda b: (b, 0, 0))],
        out_shape=[jax.ShapeDtypeStruct((B, n_leaves, leaf, FNET_WIDTH), BF16),
                   jax.ShapeDtypeStruct((B, 2 * D_MODEL, M), BF16)],
        scratch_shapes=[pltpu.VMEM((S, FNET_WIDTH), F32), pltpu.VMEM((S, FNET_WIDTH), F32),
                        pltpu.VMEM((n_leaves, 2 * leaf, FNET_WIDTH), BF16)],
        compiler_params=_params(1),
        name="dft",
    )(vr, vi, twc, tws, cs, w_f, mem, mem_g, w_xkv_t)


def _post_kernel(x_ref, o_ref, g_ref, mkv_t_ref, w_out_ref, ln_x_g_ref, w_xq_ref,
                 w_xo_ref, ln_f_g_ref, w_gu_ref, w_d_ref, fin_g_ref, y_ref, a_scr, g_scr):
    n_res, rows_per_res = g_ref.shape[1], g_ref.shape[2]
    for r in range(n_res):
        g_r = g_ref[0, r].astype(F32)
        for lt in range(FNET_WIDTH // LANES):
            g_scr[lt, pl.ds(r, rows_per_res, stride=n_res), :] = g_r[:, lt * LANES:(lt + 1) * LANES]
    g = jnp.concatenate([g_scr[lt] for lt in range(FNET_WIDTH // LANES)], axis=1).astype(BF16)

    x1 = x_ref[0] + _dot(jnp.concatenate([o_ref[0], g], axis=1), w_out_ref[...])

    hq = _rms(x1, ln_x_g_ref[...]).astype(BF16)
    q = (_dot(hq, w_xq_ref[...]) * (XATTN_HEAD_DIM ** -0.5 * LOG2_E)).astype(BF16)
    heads = []
    for hd in range(XATTN_HEADS):
        sl = slice(hd * XATTN_HEAD_DIM, (hd + 1) * XATTN_HEAD_DIM)
        s = _dot(q[:, sl], mkv_t_ref[0, sl, :])
        m = jnp.max(s, axis=-1, keepdims=True)
        p = jnp.exp2(s - m)
        l = jnp.sum(p, axis=-1, keepdims=True)
        vh_t = mkv_t_ref[0, D_MODEL + hd * XATTN_HEAD_DIM:D_MODEL + (hd + 1) * XATTN_HEAD_DIM, :]
        heads.append((_dot_nt(p.astype(BF16), vh_t) / l).astype(BF16))
    x2 = x1 + _dot(jnp.concatenate(heads, axis=-1), w_xo_ref[...])

    hf = _rms(x2, ln_f_g_ref[...]).astype(BF16)
    for c in range(D_FF // FF_CHUNK):
        sl = slice(c * FF_CHUNK, (c + 1) * FF_CHUNK)
        gate = _dot(hf, w_gu_ref[:, sl])
        up = _dot(hf, w_gu_ref[:, D_FF + c * FF_CHUNK:D_FF + (c + 1) * FF_CHUNK])
        a_scr[:, sl] = (gate * jax.nn.sigmoid(gate) * up).astype(BF16)
    x3 = x2 + _dot(a_scr[...], w_d_ref[...])
    y_ref[0] = _rms(x3, fin_g_ref[...])


def _post(x, o, g, mkv_t, w_out, ln_x_g, w_xq, w_xo, ln_f_g, w_gu, w_d, fin_g, tile):
    B, S, _ = x.shape
    M = mkv_t.shape[2]
    n_res = g.shape[1]
    tok = lambda w: pl.BlockSpec((1, tile, w), lambda b, t: (b, t, 0))
    consts = [w_out, ln_x_g, w_xq, w_xo, ln_f_g, w_gu, w_d, fin_g]
    return pl.pallas_call(
        _post_kernel,
        grid=(B, S // tile),
        in_specs=[tok(D_MODEL), tok(MLA_WIDTH),
                  pl.BlockSpec((1, n_res, tile // n_res, FNET_WIDTH), lambda b, t: (b, 0, t, 0)),
                  pl.BlockSpec((1, 2 * D_MODEL, M), lambda b, t: (b, 0, 0))]
                 + [_const_spec(c.shape) for c in consts],
        out_specs=tok(D_MODEL),
        out_shape=jax.ShapeDtypeStruct((B, S, D_MODEL), F32),
        scratch_shapes=[pltpu.VMEM((tile, D_FF), BF16), pltpu.VMEM((FNET_WIDTH // LANES, tile, LANES), F32)],
        compiler_params=_params(2),
        name="post",
    )(x, o, g, mkv_t, *consts)


def _rot_cols(w):
    half = QK_ROPE_DIM // 2
    return jnp.concatenate([-w[..., half:], w[..., :half]], axis=-1)


def _prep_weights(w_in, w_uq, w_ukv, w_fnet):
    w_cq = w_in[:, :Q_LORA_RANK]
    w_ckv = w_in[:, Q_LORA_RANK:Q_LORA_RANK + KV_LORA_RANK]
    o3 = Q_LORA_RANK + KV_LORA_RANK
    w_kr = w_in[:, o3:o3 + QK_ROPE_DIM]
    w_u = w_in[:, o3 + QK_ROPE_DIM:]
    w_kr_grp = jnp.concatenate([jnp.zeros((D_MODEL, QK_NOPE_DIM), F32), w_kr, _rot_cols(w_kr)], axis=-1)
    w_in_ext = jnp.concatenate([w_cq, w_ckv, w_kr_grp, w_u], axis=-1).astype(BF16)

    w_q_t = w_uq.T.astype(BF16)

    kv_w = w_ukv.reshape(KV_LORA_RANK, MLA_HEADS, QK_NOPE_DIM + V_HEAD_DIM)
    w_uv_t = kv_w[..., QK_NOPE_DIM:].reshape(KV_LORA_RANK, MLA_WIDTH).T.astype(BF16)

    w_f = jnp.zeros((FNET_WIDTH, FNET_WIDTH), F32)
    for gi in range(FNET_GROUPS):
        sl = slice(gi * FNET_GROUP_DIM, (gi + 1) * FNET_GROUP_DIM)
        w_f = w_f.at[sl, sl].set(w_fnet[gi])
    return w_in_ext, w_q_t, w_uv_t, w_f.astype(BF16)


def _dft_mats(n):
    idx = np.arange(n, dtype=np.int64)
    ang = ((idx[:, None] * idx[None, :]) % n) * (2.0 * np.pi / n)
    return np.cos(ang), np.sin(ang)


def _seq_dft_tables(seq):
    cs_rows, sn_rows = [], []
    n = seq
    for _ in range(DFT_LEVELS):
        ang = np.arange(n // 2) * (2.0 * np.pi / n)
        cs_rows.append(np.cos(ang))
        sn_rows.append(np.sin(ang))
        n //= 2
    rep = lambda rows: jnp.asarray(np.broadcast_to(np.concatenate(rows)[:, None], (seq - n, LANES)), F32)
    c, s = _dft_mats(n)
    leaf = jnp.asarray(np.concatenate([c, s], axis=1) * seq ** -0.5, F32).astype(BF16)
    return rep(cs_rows), rep(sn_rows), leaf


def _channel_dft():
    c, s = _dft_mats(FNET_GROUP_DIM)
    return jnp.asarray(np.concatenate([c, -s], axis=-1) * FNET_GROUP_DIM ** -0.5, F32).astype(BF16)


def _rope_tabs(seq):
    inv = 1.0 / (ROPE_BASE ** (np.arange(0, QK_ROPE_DIM, 2) / QK_ROPE_DIM))
    ang = np.arange(seq)[:, None] * inv[None, :]
    cos, sin = np.cos(ang), np.sin(ang)
    pad = lambda t: jnp.asarray(np.concatenate(
        [np.zeros((seq, QK_NOPE_DIM)), t, t, np.zeros((seq, HEAD_PAD - QK_NOPE_DIM - QK_ROPE_DIM))], axis=-1), F32)
    return pad(cos), pad(sin), jnp.asarray(cos.T, F32), jnp.asarray(sin.T, F32)


def _trunk(x, mem, w, tiles):
    S = x.shape[1]
    ctab, stab, cos_t, sin_t = _rope_tabs(S)
    twc, tws, cs = _seq_dft_tables(S)
    q_t, k, v_t, kn2, qn2, vr, vi = _proj(x, ctab, stab, cos_t, sin_t, w["ln_mix_g"], w["w_in_ext"],
                                          w["q_norm_g"], w["w_q_t"], w["kv_norm_g"], w["w_ukv"],
                                          w["w_uv_t"], w["w_cdft"], min(tiles[0], S))
    shift, safe = _score_shifts(kn2, qn2, min(tiles[1], S))
    o = _attn(q_t, k, v_t, shift, safe, min(tiles[1], S))
    g, mkv_t = _dft(vr, vi, twc, tws, cs, w["w_f"], mem, w["mem_norm_g"], w["w_xkv_t"])
    return _post(x, o, g, mkv_t, w["w_out"], w["ln_x_g"], w["w_xq"], w["w_xo"],
                 w["ln_ffn_g"], w["w_gate_up"], w["w_down"], w["final_norm_g"], min(tiles[2], S))


def kernel(x_prompt, x_sample, mem_prompt, mem_sample, ln_mix_g, w_in, q_norm_g, w_uq, kv_norm_g, w_ukv, w_fnet,
           w_out, ln_x_g, mem_norm_g, w_xq, w_xkv, w_xo, ln_ffn_g, w_gate_up, w_down, final_norm_g):
    assert ln_mix_g.shape[0] == 1, "single-layer trunk"
    w_in_ext, w_q_t, w_uv_t, w_f = _prep_weights(w_in[0], w_uq[0], w_ukv[0], w_fnet[0])
    row = lambda g: g.reshape(1, -1).astype(F32)
    w = dict(
        ln_mix_g=row(ln_mix_g[0]), w_in_ext=w_in_ext, q_norm_g=row(q_norm_g[0]), w_q_t=w_q_t,
        kv_norm_g=row(kv_norm_g[0]), w_ukv=w_ukv[0].astype(BF16), w_uv_t=w_uv_t, w_cdft=_channel_dft(), w_f=w_f,
        w_out=w_out[0].astype(BF16), ln_x_g=row(ln_x_g[0]), mem_norm_g=row(mem_norm_g[0]),
        w_xq=w_xq[0].astype(BF16), w_xkv_t=w_xkv[0].T.astype(BF16), w_xo=w_xo[0].astype(BF16),
        ln_ffn_g=row(ln_ffn_g[0]), w_gate_up=w_gate_up[0].astype(BF16), w_down=w_down[0].astype(BF16),
        final_norm_g=row(final_norm_g),
    )
    return (_trunk(x_prompt, mem_prompt, w, TILES), _trunk(x_sample, mem_sample, w, TILES))
```
